```python
import math
import jax, jax.numpy as jnp
from jax import lax
import numpy as np

D_MODEL = 1024
BATCH = 4
SEQ = 8192
DEPTH = 2

RWKV_HEADS = 8
RWKV_HEAD_DIM = 64
RWKV_WIDTH = RWKV_HEADS * RWKV_HEAD_DIM
DECAY_LORA = 64
AAA_LORA = 64
MV_LORA = 32
GATE_LORA = 128
LN_X_EPS = 64e-5
MOBA_HEADS = 8
MOBA_HEAD_DIM = 64
MOBA_WIDTH = MOBA_HEADS * MOBA_HEAD_DIM
MOBA_BLOCK = 256
MOBA_TOPK = 3
Q_CHUNK = 128
REL_BUCKETS = 32
REL_MAX_DIST = 128
N_GROUPS = 4
EXPERTS_PER_GROUP = 8
N_EXPERTS = N_GROUPS * EXPERTS_PER_GROUP
FINE_TOPK = 2
D_EXPERT = 512
MOE_CHUNK = 1024
NORM_EPS = 1e-6
QK_EPS = 1e-6
IN_WIDTH = 3 * RWKV_WIDTH + 3 * MOBA_WIDTH + 2 * D_MODEL
SPLITS = [RWKV_WIDTH, 2 * RWKV_WIDTH, 3 * RWKV_WIDTH,
          3 * RWKV_WIDTH + MOBA_WIDTH, 3 * RWKV_WIDTH + 2 * MOBA_WIDTH,
          3 * RWKV_WIDTH + 3 * MOBA_WIDTH, 3 * RWKV_WIDTH + 3 * MOBA_WIDTH + D_MODEL]

kernel_name = 'hybrid_rwkv7_moba_hmoe'


def rms_norm(x, g, eps=NORM_EPS):
    xf = x.astype(jnp.float32)
    y = xf * lax.rsqrt(jnp.mean(xf * xf, axis=-1, keepdims=True) + eps)
    return (y * g.astype(jnp.float32)).astype(x.dtype)


def token_shift(x):
    return jnp.pad(x, ((0, 0), (1, 0), (0, 0)))[:, :-1]


def t5_bucket(rel):
    n = jnp.maximum(-rel, 0)
    max_exact = REL_BUCKETS // 2
    nf = jnp.maximum(n, 1).astype(jnp.float32)
    large = max_exact + (jnp.log(nf / max_exact) / math.log(REL_MAX_DIST / max_exact)
                         * (REL_BUCKETS - max_exact)).astype(jnp.int32)
    large = jnp.minimum(large, REL_BUCKETS - 1)
    return jnp.where(n < max_exact, n, large)


def rwkv7_scan(r, w, k, v, a, b):
    B_, S_, H, N = r.shape

    def step(state, inp):
        r_t, w_t, k_t, v_t, a_t, b_t = inp
        sa = jnp.einsum('bhvk,bhk->bhv', state, a_t)
        state = (state * w_t[:, :, None, :] + sa[..., None] * b_t[:, :, None, :]
                 + v_t[..., None] * k_t[:, :, None, :])
        return state, jnp.einsum('bhvk,bhk->bhv', state, r_t)

    xs = tuple(jnp.moveaxis(t, 1, 0) for t in (r, w, k, v, a, b))
    s0 = jnp.zeros((B_, H, N, N), jnp.float32)
    _, y = lax.scan(step, s0, xs)
    return jnp.moveaxis(y, 0, 1)


def rwkv7_branch(xn, r_p, k_p, v_p, mu_rkv, mu_x, w0, w1, w2, a0, a1, a2, g1, g2,
                 k_k, k_a, r_k, lnx_g, lnx_b, v_first, vres):
    B_, S_, _ = xn.shape
    H, N = RWKV_HEADS, RWKV_HEAD_DIM
    f32 = jnp.float32
    r = r_p + (token_shift(r_p) - r_p) * mu_rkv[0]
    k = k_p + (token_shift(k_p) - k_p) * mu_rkv[1]
    v = v_p + (token_shift(v_p) - v_p) * mu_rkv[2]
    xx = token_shift(xn) - xn
    xw = xn + xx * mu_x[0]
    xa = xn + xx * mu_x[1]
    xg = xn + xx * mu_x[2]
    w = -jax.nn.softplus(-(w0 + jnp.tanh(xw @ w1) @ w2).astype(f32)) - 0.5
    decay = jnp.exp(-jnp.exp(w))
    a = jax.nn.sigmoid(a0 + (xa @ a1) @ a2)
    g = jax.nn.sigmoid(xg @ g1) @ g2
    if vres is None:
        v_first = v
    else:
        v_mu, v0, v1, v2 = vres
        xv = xn + xx * v_mu
        v = v + (v_first - v) * jax.nn.sigmoid(v0 + (xv @ v1) @ v2)
    heads = lambda t: t.reshape(B_, S_, H, N).astype(f32)
    kk = heads(k * k_k)
    kk = kk / jnp.maximum(jnp.sqrt(jnp.sum(kk * kk, axis=-1, keepdims=True)), 1e-12)
    k = k * (1.0 + (a - 1.0) * k_a)
    rh, kh, vh, ah = heads(r), heads(k), heads(v), heads(a)
    y = rwkv7_scan(rh, heads(decay), kh, vh, -kk, kk * ah)
    mean = jnp.mean(y, axis=-1, keepdims=True)
    var = jnp.mean(jnp.square(y - mean), axis=-1, keepdims=True)
    y = ((y - mean) * lax.rsqrt(var + LN_X_EPS)).reshape(B_, S_, H * N)
    y = y * lnx_g.astype(f32) + lnx_b.astype(f32)
    bonus = jnp.sum(rh * kh * r_k.astype(f32), axis=-1, keepdims=True) * vh
    y = (y + bonus.reshape(B_, S_, H * N)) * g.astype(f32)
    return y.astype(xn.dtype), v_first


def moba_branch(q, k, v, q_gain, k_gain, rel_table):
    B_, S_, _ = q.shape
    H, Dh = MOBA_HEADS, MOBA_HEAD_DIM
    f32 = jnp.float32
    q = rms_norm(q.reshape(B_, S_, H, Dh), q_gain, QK_EPS).transpose(0, 2, 1, 3) * (Dh ** -0.5)
    k = rms_norm(k.reshape(B_, S_, H, Dh), k_gain, QK_EPS).transpose(0, 2, 1, 3)
    v = v.reshape(B_, S_, H, Dh).transpose(0, 2, 1, 3)
    S_pad = -(-S_ // MOBA_BLOCK) * MOBA_BLOCK
    pad = ((0, 0), (0, 0), (0, S_pad - S_), (0, 0))
    k_pad, v_pad = jnp.pad(k, pad), jnp.pad(v, pad)
    nb = S_pad // MOBA_BLOCK
    topk = min(MOBA_TOPK, nb)
    k_blocks = k_pad.reshape(B_, H, nb, MOBA_BLOCK, Dh)
    v_blocks = v_pad.reshape(B_, H, nb, MOBA_BLOCK, Dh)
    k_mean = jnp.mean(k_blocks.astype(f32), axis=3)
    table_t = rel_table.T.astype(f32)
    b_ix = jnp.arange(B_)[:, None, None, None]
    h_ix = jnp.arange(H)[None, :, None, None]
    h_ix5 = jnp.arange(H)[None, :, None, None, None]
    offs = jnp.arange(MOBA_BLOCK)

    def chunk_fn(c):
        q0 = c * Q_CHUNK
        qc = lax.dynamic_slice_in_dim(q, q0, Q_CHUNK, axis=2)
        blk = q0 // MOBA_BLOCK
        qpos = q0 + jnp.arange(Q_CHUNK)
        sc = jnp.einsum('bhqd,bhnd->bhqn', qc.astype(f32), k_mean)
        sc = jnp.where(jnp.arange(nb) < blk, sc, -jnp.inf)
        _, idx = lax.top_k(sc, topk)
        valid = jnp.arange(topk) < blk
        kg = k_blocks[b_ix, h_ix, idx]
        vg = v_blocks[b_ix, h_ix, idx]
        kpos = idx[..., None] * MOBA_BLOCK + offs
        rb = table_t[h_ix5, t5_bucket(kpos - qpos[:, None, None])]
        ls = jnp.einsum('bhqd,bhqstd->bhqst', qc, kg).astype(f32) + rb
        ls = jnp.where(valid[:, None], ls, -jnp.inf).reshape(B_, H, Q_CHUNK, topk * MOBA_BLOCK)
        k_own = lax.dynamic_slice_in_dim(k_pad, blk * MOBA_BLOCK, MOBA_BLOCK, axis=2)
        v_own = lax.dynamic_slice_in_dim(v_pad, blk * MOBA_BLOCK, MOBA_BLOCK, axis=2)
        rel = (blk * MOBA_BLOCK + offs)[None, :] - qpos[:, None]
        lo = jnp.einsum('bhqd,bhtd->bhqt', qc, k_own).astype(f32) + table_t[:, t5_bucket(rel)]
        lo = jnp.where(rel <= 0, lo, -jnp.inf)
        p = jax.nn.softmax(jnp.concatenate([ls, lo], axis=-1), axis=-1)
        ps = p[..., :topk * MOBA_BLOCK].reshape(B_, H, Q_CHUNK, topk, MOBA_BLOCK).astype(v.dtype)
        po = p[..., topk * MOBA_BLOCK:].astype(v.dtype)
        return (jnp.einsum('bhqst,bhqstd->bhqd', ps, vg)
                + jnp.einsum('bhqt,bhtd->bhqd', po, v_own))

    outs = lax.map(chunk_fn, jnp.arange(S_ // Q_CHUNK))
    return outs.transpose(1, 0, 3, 2, 4).reshape(B_, S_, H * Dh)


def hier_moe(xn, w_grp, b_grp, w_exp, b_exp, w_gate, w_up, w_down):
    B_, S_, D = xn.shape
    t = xn.reshape(-1, D)
    T = t.shape[0]
    f32 = jnp.float32
    lg = (t @ w_grp + b_grp).astype(f32)
    pg = jax.nn.softmax(lg, axis=-1)
    g_sel = jnp.argmax(lg, axis=-1)
    p_top = jnp.take_along_axis(pg, g_sel[:, None], axis=-1)
    le = (t @ w_exp + b_exp).astype(f32).reshape(T, N_GROUPS, EXPERTS_PER_GROUP)
    le_sel = jnp.take_along_axis(le, g_sel[:, None, None], axis=1)[:, 0]
    pe = jax.nn.softmax(le_sel, axis=-1)
    v2, i2 = lax.top_k(pe, FINE_TOPK)
    gates = p_top * (v2 / jnp.sum(v2, axis=-1, keepdims=True))
    eid = g_sel[:, None] * EXPERTS_PER_GROUP + i2
    comb = jnp.sum(jax.nn.one_hot(eid, N_EXPERTS, dtype=f32) * gates[..., None], axis=1)
    chunk = math.gcd(T, MOE_CHUNK)

    def chunk_fn(args):
        tc, cc = args
        h = jax.nn.silu(jnp.einsum('td,edh->teh', tc, w_gate)) * jnp.einsum('td,edh->teh', tc, w_up)
        h = h * cc[..., None].astype(h.dtype)
        return jnp.einsum('teh,ehd->td', h, w_down)

    y = lax.map(chunk_fn, (t.reshape(-1, chunk, D), comb.reshape(-1, chunk, N_EXPERTS)))
    return y.reshape(B_, S_, D)


def setup_inputs(seed: int = 0) -> dict:
    key = jax.random.key(seed)
    ks = iter(jax.random.split(key, 48))
    f32 = jnp.float32

    def nrm(shape, scale):
        return scale * jax.random.normal(next(ks), shape, f32)

    def uni(shape, lo, hi):
        return jax.random.uniform(next(ks), shape, f32, lo, hi)

    L, D, CR, CM = DEPTH, D_MODEL, RWKV_WIDTH, MOBA_WIDTH
    LV = DEPTH - 1
    return {
        'x': nrm((BATCH, SEQ, D), 1.0),
        'norm_mix_g': 1.0 + nrm((L, D), 0.02),
        'norm_ffn_g': 1.0 + nrm((L, D), 0.02),
        'w_in': nrm((L, D, IN_WIDTH), D ** -0.5),
        'mu_rkv': uni((L, 3, CR), 0.0, 1.0),
        'mu_x': uni((L, 3, D), 0.0, 1.0),
        'w0': uni((L, CR), -6.5, -1.5),
        'w1': nrm((L, D, DECAY_LORA), D ** -0.5),
        'w2': nrm((L, DECAY_LORA, CR), 0.1 * DECAY_LORA ** -0.5),
        'a0': nrm((L, CR), 0.1),
        'a1': nrm((L, D, AAA_LORA), D ** -0.5),
        'a2': nrm((L, AAA_LORA, CR), 0.5 * AAA_LORA ** -0.5),
        'g1': nrm((L, D, GATE_LORA), D ** -0.5),
        'g2': nrm((L, GATE_LORA, CR), GATE_LORA ** -0.5),
        'k_k': 0.85 + nrm((L, CR), 0.05),
        'k_a': 1.0 + nrm((L, CR), 0.05),
        'r_k': nrm((L, RWKV_HEADS, RWKV_HEAD_DIM), 0.1),
        'lnx_g': 1.0 + nrm((L, CR), 0.02),
        'lnx_b': nrm((L, CR), 0.02),
        'vres_mu': uni((LV, D), 0.0, 1.0),
        'vres_v0': nrm((LV, CR), 0.1),
        'vres_v1': nrm((LV, D, MV_LORA), D ** -0.5),
        'vres_v2': nrm((LV, MV_LORA, CR), 0.5 * MV_LORA ** -0.5),
        'q_gain': 1.0 + nrm((L, MOBA_HEAD_DIM), 0.02),
        'k_gain': 1.0 + nrm((L, MOBA_HEAD_DIM), 0.02),
        'rel_bias': nrm((REL_BUCKETS, MOBA_HEADS), 0.5),
        'w_br_rwkv': nrm((L, CR, D), CR ** -0.5),
        'w_br_moba': nrm((L, CM, D), CM ** -0.5),
        'w_out': nrm((L, D, D), D ** -0.5),
        'w_grp': nrm((L, D, N_GROUPS), D ** -0.5),
        'b_grp': nrm((L, N_GROUPS), 0.01),
        'w_exp': nrm((L, D, N_EXPERTS), D ** -0.5),
        'b_exp': nrm((L, N_EXPERTS), 0.01),
        'w_gate': nrm((L, N_EXPERTS, D, D_EXPERT), D ** -0.5),
        'w_up': nrm((L, N_EXPERTS, D, D_EXPERT), D ** -0.5),
        'w_down': nrm((L, N_EXPERTS, D_EXPERT, D), D_EXPERT ** -0.5),
    }


def reference(x, norm_mix_g, norm_ffn_g, w_in, mu_rkv, mu_x, w0, w1, w2, a0, a1, a2, g1, g2,
              k_k, k_a, r_k, lnx_g, lnx_b, vres_mu, vres_v0, vres_v1, vres_v2,
              q_gain, k_gain, rel_bias, w_br_rwkv, w_br_moba, w_out,
              w_grp, b_grp, w_exp, b_exp, w_gate, w_up, w_down):
    v_first = None
    for l in range(DEPTH):
        xn = rms_norm(x, norm_mix_g[l])
        proj = xn @ w_in[l]
        r_p, k_p, v_p, q_m, k_m, v_m, gate_r, gate_m = jnp.split(proj, SPLITS, axis=-1)
        vres = None if l == 0 else (vres_mu[l - 1], vres_v0[l - 1], vres_v1[l - 1], vres_v2[l - 1])
        y_r, v_first = rwkv7_branch(xn, r_p, k_p, v_p, mu_rkv[l], mu_x[l], w0[l], w1[l], w2[l],
                                    a0[l], a1[l], a2[l], g1[l], g2[l], k_k[l], k_a[l], r_k[l],
                                    lnx_g[l], lnx_b[l], v_first, vres)
        y_m = moba_branch(q_m, k_m, v_m, q_gain[l], k_gain[l], rel_bias)
        h = (jax.nn.sigmoid(gate_r) * (y_r @ w_br_rwkv[l])
             + jax.nn.sigmoid(gate_m) * (y_m @ w_br_moba[l]))
        x = x + h @ w_out[l]
        xn = rms_norm(x, norm_ffn_g[l])
        x = x + hier_moe(xn, w_grp[l], b_grp[l], w_exp[l], b_exp[l], w_gate[l], w_up[l], w_down[l])
    return x
```

```python
import functools
import math

import jax
import jax.numpy as jnp
from jax import lax
from jax.experimental import pallas as pl
from jax.experimental.pallas import tpu as pltpu

F32 = jnp.float32
BF16 = jnp.bfloat16

HEADS = 8
HEAD_DIM = 64
WIDTH = HEADS * HEAD_DIM
DECAY_LORA, AAA_LORA, MV_LORA, GATE_LORA = 64, 64, 32, 128
LN_X_EPS = 64e-5
MOBA_BLOCK = 256
MOBA_TOPK = 3
REL_BUCKETS = 32
REL_MAX_DIST = 128
N_GROUPS = 4
EXPERTS_PER_GROUP = 8
N_EXPERTS = N_GROUPS * EXPERTS_PER_GROUP
NORM_EPS = 1e-6
QK_EPS = 1e-6

CHUNK = 64
SCAN_BLOCK = 512
SCAN_GROUP = 8
LORA_PAD = 512
ROW_TILE = 256
MOE_TILE = 256
NEG_BIG = -1e30
VMEM_LIMIT = 48 * 1024 * 1024


def _cparams(*sem):
    return pltpu.CompilerParams(dimension_semantics=sem, vmem_limit_bytes=VMEM_LIMIT)


def _dot(a, b):
    return jnp.dot(a.astype(BF16), b.astype(BF16), preferred_element_type=F32)


def _dot_nt(a, b):
    return lax.dot_general(a.astype(BF16), b.astype(BF16), (((1,), (1,)), ((), ())),
                           preferred_element_type=F32)


def _dot_tn(a, b):
    return lax.dot_general(a.astype(BF16), b.astype(BF16), (((0,), (0,)), ((), ())),
                           preferred_element_type=F32)


def _split(x):
    hi = x.astype(BF16)
    lo = (x - hi.astype(F32)).astype(BF16)
    return hi, lo


def _dot_lsplit(a, b_exact):
    hi, lo = _split(a)
    return (jnp.dot(hi, b_exact, preferred_element_type=F32)
            + jnp.dot(lo, b_exact, preferred_element_type=F32))


def _dot_rsplit(a_exact, b):
    hi, lo = _split(b)
    return (jnp.dot(a_exact, hi, preferred_element_type=F32)
            + jnp.dot(a_exact, lo, preferred_element_type=F32))


def _dot_x3(a, b):
    ah, al = _split(a)
    bh, bl = _split(b)
    return (jnp.dot(ah, bh, preferred_element_type=F32)
            + jnp.dot(al, bh, preferred_element_type=F32)
            + jnp.dot(ah, bl, preferred_element_type=F32))


def _sigmoid(x):
    return 1.0 / (1.0 + jnp.exp(-x))


def _norm_proj_kernel(x_ref, g_ref, w_ref, o_ref, xn_ref):
    @pl.when(pl.program_id(1) == 0)
    def _():
        x = x_ref[...]
        ms = jnp.mean(x * x, axis=-1, keepdims=True)
        xn_ref[...] = (x * lax.rsqrt(ms + NORM_EPS) * g_ref[...]).astype(BF16)

    o_ref[...] = jnp.dot(xn_ref[...], w_ref[...], preferred_element_type=F32)


def _norm_proj(x, g, w, tm=512, tn=1024):
    m, k = x.shape
    n = w.shape[1]
    return pl.pallas_call(
        _norm_proj_kernel,
        grid=(m // tm, n // tn),
        in_specs=[pl.BlockSpec((tm, k), lambda i, j: (i, 0)),
                  pl.BlockSpec((1, k), lambda i, j: (0, 0)),
                  pl.BlockSpec((k, tn), lambda i, j: (0, j))],
        out_specs=pl.BlockSpec((tm, tn), lambda i, j: (i, j)),
        out_shape=jax.ShapeDtypeStruct((m, n), F32),
        scratch_shapes=[pltpu.VMEM((tm, k), BF16)],
        compiler_params=_cparams("parallel", "arbitrary"),
    )(x, g.reshape(1, k), w)


def _shift_rows(x, halo_ref, first):
    prev = jnp.where(first, 0.0, halo_ref[7:8, :])
    rolled = pltpu.roll(x, 1, axis=0)
    row = lax.broadcasted_iota(jnp.int32, x.shape, 0)
    return jnp.where(row == 0, prev, rolled)


def _rwkv_prep_kernel(has_vres, tiles_per_seq, *refs):
    (rp_ref, kp_ref, vp_ref, p1_ref, p2_ref, rph_ref, kph_ref, vph_ref, p2h_ref,
     mu_ref, vec_ref, wwa_ref, g2_ref, v2_ref, ones_ref) = refs[:15]
    refs = refs[15:]
    if has_vres:
        vfirst_ref = refs[0]
        refs = refs[1:]
    r_o, lw_o, k_o, v_o, kk_o, b_o, g_o, bonus_o = refs

    first = (pl.program_id(0) % tiles_per_seq) == 0
    rp, kp, vp = rp_ref[...], kp_ref[...], vp_ref[...]
    r = rp + (_shift_rows(rp, rph_ref, first) - rp) * mu_ref[0:1, :]
    k = kp + (_shift_rows(kp, kph_ref, first) - kp) * mu_ref[1:2, :]
    v = vp + (_shift_rows(vp, vph_ref, first) - vp) * mu_ref[2:3, :]
    lo = p1_ref[...] + _shift_rows(p2_ref[...], p2h_ref, first)

    w0, a0, k_k, k_a = vec_ref[0:1, :], vec_ref[1:2, :], vec_ref[2:3, :], vec_ref[3:4, :]
    r_k, v0 = vec_ref[4:5, :], vec_ref[5:6, :]

    h_wa = lo[:, 0:128]
    lane = lax.broadcasted_iota(jnp.int32, h_wa.shape, 1)
    h_wa = jnp.where(lane < DECAY_LORA, jnp.tanh(h_wa), h_wa)
    wa = _dot(h_wa, wwa_ref[...])
    z = -(w0 + wa[:, :WIDTH])
    softplus = jnp.maximum(z, 0.0) + jnp.log(1.0 + jnp.exp(-jnp.abs(z)))
    logw = -jnp.exp(-softplus - 0.5)
    a = _sigmoid(a0 + wa[:, WIDTH:])
    g = _dot(_sigmoid(lo[:, 128:256]), g2_ref[...])
    if has_vres:
        gate_v = _sigmoid(v0 + _dot(lo[:, 256:384], v2_ref[...]))
        v = v + (vfirst_ref[...] - v) * gate_v

    ones_bd = ones_ref[...]
    kk = k * k_k
    ss = _dot_lsplit(kk * kk, ones_bd)
    kk = kk / jnp.maximum(jnp.sqrt(ss), 1e-12)
    k = k * (1.0 + (a - 1.0) * k_a)
    bonus = _dot_lsplit(r * k * r_k, ones_bd) * v

    r_o[...] = r
    lw_o[...] = logw
    k_o[...] = k
    v_o[...] = v
    kk_o[...] = kk
    b_o[...] = kk * a
    g_o[...] = g
    bonus_o[...] = bonus


def _rwkv_prep(proj, seq, mu_rkv, vecs, wwa, g2, v2p, ones_bd, v_first):
    t = proj.shape[0]
    ts = ROW_TILE
    has_vres = v_first is not None
    col = lambda c: pl.BlockSpec((ts, WIDTH), lambda i, c=c: (i, c))
    halo = lambda c: pl.BlockSpec((8, WIDTH), lambda i, c=c: (jnp.maximum(i * (ts // 8) - 1, 0), c))
    full = lambda shape: pl.BlockSpec(shape, lambda i: tuple(0 for _ in shape))
    p1c = 5120 // WIDTH
    in_specs = [col(0), col(1), col(2), col(p1c), col(p1c + 1),
                halo(0), halo(1), halo(2), halo(p1c + 1),
                full((3, WIDTH)), full((8, WIDTH)), full((128, 2 * WIDTH)),
                full((128, WIDTH)), full((128, WIDTH)), full((WIDTH, WIDTH))]
    args = [proj] * 9 + [mu_rkv, vecs, wwa, g2, v2p, ones_bd]
    if has_vres:
        in_specs.append(pl.BlockSpec((ts, WIDTH), lambda i: (i, 0)))
        args.append(v_first)
    out = pl.pallas_call(
        functools.partial(_rwkv_prep_kernel, has_vres, seq // ts),
        grid=(t // ts,),
        in_specs=in_specs,
        out_specs=[pl.BlockSpec((ts, WIDTH), lambda i: (i, 0))] * 8,
        out_shape=[jax.ShapeDtypeStruct((t, WIDTH), F32)] * 8,
        compiler_params=_cparams("parallel"),
    )(*args)
    return out


def _rwkv_chunk_prep_kernel(r_ref, lw_ref, k_ref, v_ref, kk_ref, b_ref,
                            atp_ref, rt_ref, arb_ref, bh_ref, w2_ref, y2_ref, z2_ref, gc_ref):
    c_len = CHUNK
    n_chunks = r_ref.shape[1] // c_len
    row = lax.broadcasted_iota(jnp.int32, (c_len, c_len), 0)
    col = lax.broadcasted_iota(jnp.int32, (c_len, c_len), 1)
    strict = row > col
    incl = row >= col
    tril = incl.astype(BF16)
    m16 = (row // 16) == (col // 16)
    m32 = (row // 32) == (col // 32)
    eye = (row == col).astype(F32)
    row8 = lax.broadcasted_iota(jnp.int32, (n_chunks, HEAD_DIM), 0)

    def body(c, gc_acc):
        sl = pl.ds(pl.multiple_of(c * c_len, c_len), c_len)
        r, lw, k, v = r_ref[0, sl, :], lw_ref[0, sl, :], k_ref[0, sl, :], v_ref[0, sl, :]
        kk, b = kk_ref[0, sl, :], b_ref[0, sl, :]
        cum = _dot_rsplit(tril, lw)
        e_pos = jnp.exp(cum)
        e_neg = jnp.exp(-cum)
        at = -kk * jnp.exp(cum - lw)
        rt = r * e_pos
        bt = b * e_neg
        kt = k * e_neg
        aab = jnp.where(strict, _dot_nt(at, bt), 0.0)
        aak = jnp.where(strict, _dot_nt(at, kt), 0.0)
        arb = jnp.where(incl, _dot_nt(rt, bt), 0.0)
        ark = jnp.where(incl, _dot_nt(rt, kt), 0.0)
        d1 = jnp.where(m16, aab, 0.0)
        d2 = _dot(d1, d1)
        d4 = _dot(d2, d2)
        d8 = _dot(d4, d4)
        tinv = _dot(eye + d1, eye + d2)
        tinv = _dot(tinv, eye + d4)
        tinv = _dot(tinv, eye + d8)
        e32 = jnp.where(m32 & jnp.logical_not(m16), aab, 0.0)
        tinv = tinv + _dot(_dot(tinv, e32), tinv)
        e64 = jnp.where(m32, 0.0, aab)
        tinv = tinv + _dot(_dot(tinv, e64), tinv)

        g_last = e_pos[c_len - 1:c_len, :]
        atp_ref[0, sl, :] = _dot(tinv, at)
        rt_ref[0, sl, :] = rt
        arb_ref[0, sl, :] = arb
        bh_ref[0, sl, :] = bt * g_last
        w2_ref[0, sl, :] = _dot(tinv, _dot(aak, v))
        y2_ref[0, sl, :] = _dot(ark, v)
        z2_ref[0, sl, :] = _dot_tn(v, kt * g_last)
        return jnp.where(row8 == c, g_last, gc_acc)

    gc_ref[0] = lax.fori_loop(0, n_chunks, body, jnp.zeros((n_chunks, HEAD_DIM), F32))


def _rwkv_scan_kernel(atp_ref, rt_ref, arb_ref, bh_ref, w2_ref, y2_ref, z2_ref, gc_ref,
                      y_ref, s_ref):
    c_len = CHUNK
    n_chunks = atp_ref.shape[1] // c_len
    group = atp_ref.shape[0]

    @pl.when(pl.program_id(1) == 0)
    def _():
        s_ref[...] = jnp.zeros_like(s_ref)

    def body(c, carry):
        sl = pl.ds(pl.multiple_of(c * c_len, c_len), c_len)
        for gi in range(group):
            state = s_ref[gi]
            u = _dot_nt(atp_ref[gi, sl, :], state) + w2_ref[gi, sl, :]
            y = (_dot_nt(rt_ref[gi, sl, :], state) + _dot(arb_ref[gi, sl, :], u)
                 + y2_ref[gi, sl, :])
            y_ref[gi, sl, :] = y
            gc = gc_ref[gi, pl.ds(c, 1), :]
            s_ref[gi] = state * gc + _dot_tn(u, bh_ref[gi, sl, :]) + z2_ref[gi, sl, :]
        return carry

    lax.fori_loop(0, n_chunks, body, 0)


def _rwkv_scan(r, lw, k, v, kk, b):
    bh, s, n = r.shape
    sb = min(SCAN_BLOCK, s)
    nc = sb // CHUNK
    blk = pl.BlockSpec((1, sb, n), lambda i, j: (i, j, 0))
    gblk = pl.BlockSpec((1, nc, n), lambda i, j: (i, j, 0))
    outs = pl.pallas_call(
        _rwkv_chunk_prep_kernel,
        grid=(bh, s // sb),
        in_specs=[blk] * 6,
        out_specs=[blk] * 7 + [gblk],
        out_shape=[jax.ShapeDtypeStruct((bh, s, n), F32)] * 7
        + [jax.ShapeDtypeStruct((bh, s // CHUNK, n), F32)],
        compiler_params=_cparams("parallel", "parallel"),
    )(r, lw, k, v, kk, b)
    grp = SCAN_GROUP
    blk = pl.BlockSpec((grp, sb, n), lambda i, j: (i, j, 0))
    gblk = pl.BlockSpec((grp, nc, n), lambda i, j: (i, j, 0))
    return pl.pallas_call(
        _rwkv_scan_kernel,
        grid=(bh // grp, s // sb),
        in_specs=[blk] * 7 + [gblk],
        out_specs=blk,
        out_shape=jax.ShapeDtypeStruct((bh, s, n), F32),
        scratch_shapes=[pltpu.VMEM((grp, n, n), F32)],
        compiler_params=_cparams("parallel", "arbitrary"),
    )(*outs)


def _moba_prep_kernel(q_ref, k_ref, gain_ref, ones_ref, qo_ref, ko_ref, km_ref):
    ones_bd = ones_ref[...]
    q = q_ref[...]
    k = k_ref[...]
    inv_d = 1.0 / HEAD_DIM
    qn = q * lax.rsqrt(_dot_lsplit(q * q, ones_bd) * inv_d + QK_EPS) * gain_ref[0:1, :]
    kn = k * lax.rsqrt(_dot_lsplit(k * k, ones_bd) * inv_d + QK_EPS) * gain_ref[1:2, :]
    qo_ref[...] = qn * (HEAD_DIM ** -0.5)
    ko_ref[...] = kn
    km_ref[0] = jnp.mean(kn, axis=0, keepdims=True)


def _moba_prep(proj, gains, ones_bd):
    t = proj.shape[0]
    ts = MOBA_BLOCK
    return pl.pallas_call(
        _moba_prep_kernel,
        grid=(t // ts,),
        in_specs=[pl.BlockSpec((ts, WIDTH), lambda i: (i, 3)),
                  pl.BlockSpec((ts, WIDTH), lambda i: (i, 4)),
                  pl.BlockSpec((2, WIDTH), lambda i: (0, 0)),
                  pl.BlockSpec((WIDTH, WIDTH), lambda i: (0, 0))],
        out_specs=[pl.BlockSpec((ts, WIDTH), lambda i: (i, 0)),
                   pl.BlockSpec((ts, WIDTH), lambda i: (i, 0)),
                   pl.BlockSpec((1, 1, WIDTH), lambda i: (i, 0, 0))],
        out_shape=[jax.ShapeDtypeStruct((t, WIDTH), F32),
                   jax.ShapeDtypeStruct((t, WIDTH), F32),
                   jax.ShapeDtypeStruct((t // ts, 1, WIDTH), F32)],
        compiler_params=_cparams("parallel"),
    )(proj, proj, gains, ones_bd)


def _gating_kernel(q_ref, km_ref, o_ref):
    o_ref[...] = _dot_x3(q_ref[...], km_ref[0])


def _gating_scores(qf, km_mat, batch):
    t = qf.shape[0]
    ts = ROW_TILE
    per = t // batch // ts
    width = km_mat.shape[2]
    return pl.pallas_call(
        _gating_kernel,
        grid=(batch, per),
        in_specs=[pl.BlockSpec((ts, WIDTH), lambda b, i: (b * per + i, 0)),
                  pl.BlockSpec((1, WIDTH, width), lambda b, i: (b, 0, 0))],
        out_specs=pl.BlockSpec((ts, width), lambda b, i: (b * per + i, 0)),
        out_shape=jax.ShapeDtypeStruct((t, width), F32),
        compiler_params=_cparams("parallel", "parallel"),
    )(qf, km_mat)


def _rel_bias_kernel(tab_ref, o_ref):
    h = pl.program_id(0)
    blk = MOBA_BLOCK
    row = lax.broadcasted_iota(jnp.int32, (blk, blk), 0)
    col = lax.broadcasted_iota(jnp.int32, (blk, blk), 1)
    max_exact = REL_BUCKETS // 2

    def lookup(n):
        nf = jnp.maximum(n, 1).astype(F32)
        large = max_exact + (jnp.log(nf / max_exact) / math.log(REL_MAX_DIST / max_exact)
                             * (REL_BUCKETS - max_exact)).astype(jnp.int32)
        bucket = jnp.where(n < max_exact, n, jnp.minimum(large, REL_BUCKETS - 1))
        out = jnp.zeros((blk, blk), F32)
        for bkt in range(REL_BUCKETS):
            out = jnp.where(bucket == bkt, tab_ref[bkt, h], out)
        return out

    dist = row - col
    o_ref[0, 0] = jnp.where(dist >= 0, lookup(jnp.maximum(dist, 0)), NEG_BIG)
    o_ref[0, 1] = lookup(dist + blk)
    o_ref[0, 2] = lookup(dist + 2 * blk)


def _rel_bias_tiles(rel_bias):
    return pl.pallas_call(
        _rel_bias_kernel,
        grid=(HEADS,),
        in_specs=[pl.BlockSpec(memory_space=pltpu.SMEM)],
        out_specs=pl.BlockSpec((1, 3, MOBA_BLOCK, MOBA_BLOCK), lambda h: (h, 0, 0, 0)),
        out_shape=jax.ShapeDtypeStruct((HEADS, 3, MOBA_BLOCK, MOBA_BLOCK), F32),
        compiler_params=_cparams("parallel"),
    )(rel_bias)


def _moba_attn_kernel(q_ref, k_ref, v_ref, sc_ref, bias_ref, o_ref):
    blk = MOBA_BLOCK
    i = pl.program_id(1)
    nb = sc_ref.shape[2]
    q = q_ref[0]

    col = lax.broadcasted_iota(jnp.int32, (blk, nb), 1)
    past = col < i
    scm = jnp.where(past, sc_ref[0], -jnp.inf)
    sel = jnp.zeros((blk, nb), jnp.bool_)
    for _ in range(MOBA_TOPK):
        top = jnp.max(scm, axis=1, keepdims=True)
        idx = jnp.min(jnp.where(scm == top, col, nb), axis=1, keepdims=True)
        pick = col == idx
        sel = jnp.logical_or(sel, pick)
        scm = jnp.where(pick, -jnp.inf, scm)
    sel = jnp.where(jnp.logical_and(sel, past), 1.0, 0.0).astype(BF16)
    brow = lax.broadcasted_iota(jnp.int32, (nb, blk), 0)

    own = pl.ds(pl.multiple_of(i * blk, blk), blk)
    s = _dot_nt(q, k_ref[0, own, :]) + bias_ref[0, 0]
    m0 = jnp.max(s, axis=1, keepdims=True)
    p = jnp.exp(s - m0)
    l0 = jnp.sum(p, axis=1, keepdims=True)
    acc0 = _dot(p, v_ref[0, own, :])

    def body(j, carry):
        m, l, acc = carry
        sl = pl.ds(pl.multiple_of(j * blk, blk), blk)
        chosen = jnp.dot(sel, (brow == j).astype(BF16), preferred_element_type=F32)
        tile = jnp.where(j == i - 1, 1, 2)
        s = _dot_nt(q, k_ref[0, sl, :]) + bias_ref[0, tile]
        s = jnp.where(chosen > 0.5, s, NEG_BIG)
        m_new = jnp.maximum(m, jnp.max(s, axis=1, keepdims=True))
        alpha = jnp.exp(m - m_new)
        p = jnp.exp(s - m_new)
        l = alpha * l + jnp.sum(p, axis=1, keepdims=True)
        acc = alpha * acc + _dot(p, v_ref[0, sl, :])
        return m_new, l, acc

    m, l, acc = lax.fori_loop(0, i, body, (m0, l0, acc0))
    o_ref[0] = acc / l


def _moba_attn(q, k, v, sc, bias):
    bh, s, dh = q.shape
    nb = sc.shape[2]
    blk = MOBA_BLOCK
    return pl.pallas_call(
        _moba_attn_kernel,
        grid=(bh, s // blk),
        in_specs=[pl.BlockSpec((1, blk, dh), lambda b, i: (b, i, 0)),
                  pl.BlockSpec((1, s, dh), lambda b, i: (b, 0, 0)),
                  pl.BlockSpec((1, s, dh), lambda b, i: (b, 0, 0)),
                  pl.BlockSpec((1, blk, nb), lambda b, i: (b, i, 0)),
                  pl.BlockSpec((1, 3, blk, blk), lambda b, i: (b % HEADS, 0, 0, 0))],
        out_specs=pl.BlockSpec((1, blk, dh), lambda b, i: (b, i, 0)),
        out_shape=jax.ShapeDtypeStruct((bh, s, dh), F32),
        compiler_params=_cparams("parallel", "parallel"),
    )(q, k, v, sc, bias)


def _merge_router_kernel(x_ref, ys_ref, bonus_ref, g_ref, ym_ref, gr_ref, gm_ref,
                         lnx_ref, ones_ref, wr_ref, wm_ref, wo_ref, gf_ref, wrt_ref, brt_ref,
                         xo_ref, xn_ref, route_ref):
    ones_bd = ones_ref[...]
    inv_n = 1.0 / HEAD_DIM
    y = ys_ref[...]
    mean = _dot_lsplit(y, ones_bd) * inv_n
    d = y - mean
    var = _dot_lsplit(d * d, ones_bd) * inv_n
    yr = d * lax.rsqrt(var + LN_X_EPS) * lnx_ref[0:1, :] + lnx_ref[1:2, :]
    yr = (yr + bonus_ref[...]) * g_ref[...]
    h = (_sigmoid(gr_ref[...]) * _dot(yr, wr_ref[...])
         + _sigmoid(gm_ref[...]) * _dot(ym_ref[...], wm_ref[...]))
    x = x_ref[...] + _dot(h, wo_ref[...])
    xo_ref[...] = x

    ms = jnp.mean(x * x, axis=-1, keepdims=True)
    xn = x * lax.rsqrt(ms + NORM_EPS) * gf_ref[...]
    xn_ref[...] = xn

    logits = _dot_x3(xn, wrt_ref[...]) + brt_ref[...]
    lane = lax.broadcasted_iota(jnp.int32, logits.shape, 1)
    big = logits.shape[1]
    is_grp = lane < N_GROUPS
    lg = jnp.where(is_grp, logits, -jnp.inf)
    mg = jnp.max(lg, axis=1, keepdims=True)
    g_sel = jnp.min(jnp.where(lg == mg, lane, big), axis=1, keepdims=True)
    p_top = 1.0 / jnp.sum(jnp.where(is_grp, jnp.exp(logits - mg), 0.0), axis=1, keepdims=True)
    lo_lane = N_GROUPS + g_sel * EXPERTS_PER_GROUP
    in_grp = jnp.logical_and(lane >= lo_lane, lane < lo_lane + EXPERTS_PER_GROUP)
    le = jnp.where(in_grp, logits, -jnp.inf)
    m1 = jnp.max(le, axis=1, keepdims=True)
    i1 = jnp.min(jnp.where(le == m1, lane, big), axis=1, keepdims=True)
    le2 = jnp.where(lane == i1, -jnp.inf, le)
    m2 = jnp.max(le2, axis=1, keepdims=True)
    i2 = jnp.min(jnp.where(le2 == m2, lane, big), axis=1, keepdims=True)
    e2 = jnp.exp(m2 - m1)
    gate1 = p_top / (1.0 + e2)
    gate2 = p_top * e2 / (1.0 + e2)
    route = jnp.where(lane == 0, (i1 - N_GROUPS).astype(F32),
                      jnp.where(lane == 1, (i2 - N_GROUPS).astype(F32),
                                jnp.where(lane == 2, gate1, jnp.where(lane == 3, gate2, 0.0))))
    route_ref[...] = route


def _merge_router(x, ys, bonus, g, ym, proj, lnx, ones_bd, wr, wm, wo, gf, wrt, brt):
    t, d = x.shape
    tm = ROW_TILE
    row = lambda w: pl.BlockSpec((tm, w), lambda i: (i, 0))
    full = lambda shape: pl.BlockSpec(shape, lambda i: tuple(0 for _ in shape))
    return pl.pallas_call(
        _merge_router_kernel,
        grid=(t // tm,),
        in_specs=[row(d), row(WIDTH), row(WIDTH), row(WIDTH), row(WIDTH),
                  pl.BlockSpec((tm, d), lambda i: (i, 3)),
                  pl.BlockSpec((tm, d), lambda i: (i, 4)),
                  full((2, WIDTH)), full((WIDTH, WIDTH)), full((WIDTH, d)), full((WIDTH, d)),
                  full((d, d)), full((1, d)), full((d, 128)), full((1, 128))],
        out_specs=[row(d), row(d), row(128)],
        out_shape=[jax.ShapeDtypeStruct((t, d), F32), jax.ShapeDtypeStruct((t, d), F32),
                   jax.ShapeDtypeStruct((t, 128), F32)],
        compiler_params=_cparams("parallel"),
    )(x, ys, bonus, g, ym, proj, proj, lnx, ones_bd, wr, wm, wo, gf, wrt, brt)


def _moe_ffn_kernel(texp_ref, rtok_ref, nused_ref, x_hbm, wg_ref, wu_ref, wd_ref, o_ref,
                    xbuf, sem):
    i = pl.program_id(0)
    tm = xbuf.shape[0]

    def row_copy(r):
        tok = rtok_ref[i * tm + r]
        return pltpu.make_async_copy(x_hbm.at[pl.ds(tok, 1), :], xbuf.at[pl.ds(r, 1), :], sem)

    @pl.when(i < nused_ref[0])
    def _():
        def start(r, c):
            row_copy(r).start()
            return c

        def wait(r, c):
            row_copy(r).wait()
            return c

        lax.fori_loop(0, tm, start, 0)
        lax.fori_loop(0, tm, wait, 0)
        xb = xbuf[...].astype(BF16)
        hg = jnp.dot(xb, wg_ref[0], preferred_element_type=F32)
        hu = jnp.dot(xb, wu_ref[0], preferred_element_type=F32)
        h = hg * _sigmoid(hg) * hu
        o_ref[...] = jnp.dot(h.astype(BF16), wd_ref[0], preferred_element_type=F32)

    @pl.when(i >= nused_ref[0])
    def _():
        o_ref[...] = jnp.zeros_like(o_ref)


def _moe_ffn(tile_expert, row_token, nused, xn, wg, wu, wd):
    n_rows = row_token.shape[0]
    tm = MOE_TILE
    d = xn.shape[1]
    dh = wg.shape[2]
    grid_spec = pltpu.PrefetchScalarGridSpec(
        num_scalar_prefetch=3,
        grid=(n_rows // tm,),
        in_specs=[pl.BlockSpec(memory_space=pl.ANY),
                  pl.BlockSpec((1, d, dh), lambda i, te, rt, nu: (te[i], 0, 0)),
                  pl.BlockSpec((1, d, dh), lambda i, te, rt, nu: (te[i], 0, 0)),
                  pl.BlockSpec((1, dh, d), lambda i, te, rt, nu: (te[i], 0, 0))],
        out_specs=pl.BlockSpec((tm, d), lambda i, te, rt, nu: (i, 0)),
        scratch_shapes=[pltpu.VMEM((tm, d), F32), pltpu.SemaphoreType.DMA(())],
    )
    return pl.pallas_call(
        _moe_ffn_kernel,
        grid_spec=grid_spec,
        out_shape=jax.ShapeDtypeStruct((n_rows, d), F32),
        compiler_params=_cparams("arbitrary"),
    )(tile_expert, row_token, nused, xn, wg, wu, wd)


def _moe_combine_kernel(pos_ref, ys_hbm, x_ref, route_ref, o_ref, buf0, buf1, sem):
    i = pl.program_id(0)
    tm = buf0.shape[0]

    def row_copy(r, j, buf):
        p = pos_ref[(i * tm + r) * 2 + j]
        return pltpu.make_async_copy(ys_hbm.at[pl.ds(p, 1), :], buf.at[pl.ds(r, 1), :], sem)

    def start(r, c):
        row_copy(r, 0, buf0).start()
        row_copy(r, 1, buf1).start()
        return c

    def wait(r, c):
        row_copy(r, 0, buf0).wait()
        row_copy(r, 1, buf1).wait()
        return c

    lax.fori_loop(0, tm, start, 0)
    lax.fori_loop(0, tm, wait, 0)
    route = route_ref[...]
    o_ref[...] = x_ref[...] + route[:, 2:3] * buf0[...] + route[:, 3:4] * buf1[...]


def _moe_combine(pos, ys, x, route):
    t, d = x.shape
    tm = ROW_TILE
    grid_spec = pltpu.PrefetchScalarGridSpec(
        num_scalar_prefetch=1,
        grid=(t // tm,),
        in_specs=[pl.BlockSpec(memory_space=pl.ANY),
                  pl.BlockSpec((tm, d), lambda i, p: (i, 0)),
                  pl.BlockSpec((tm, 128), lambda i, p: (i, 0))],
        out_specs=pl.BlockSpec((tm, d), lambda i, p: (i, 0)),
        scratch_shapes=[pltpu.VMEM((tm, d), F32), pltpu.VMEM((tm, d), F32),
                        pltpu.SemaphoreType.DMA(())],
    )
    return pl.pallas_call(
        _moe_combine_kernel,
        grid_spec=grid_spec,
        out_shape=jax.ShapeDtypeStruct((t, d), F32),
        compiler_params=_cparams("arbitrary"),
    )(pos, ys, x, route)


def _moe_plan(route, n_tokens):
    tm = MOE_TILE
    e_flat = route[:, 0:2].astype(jnp.int32).reshape(-1)
    n_assign = e_flat.shape[0]
    n_rows = n_assign + N_EXPERTS * tm
    order = jnp.argsort(e_flat, stable=True).astype(jnp.int32)
    counts = jnp.zeros((N_EXPERTS,), jnp.int32).at[e_flat].add(1)
    padded = ((counts + tm - 1) // tm) * tm
    pend = jnp.cumsum(padded)
    pstart = pend - padded
    ustart = jnp.cumsum(counts) - counts
    sorted_e = e_flat[order]
    dest_sorted = pstart[sorted_e] + jnp.arange(n_assign, dtype=jnp.int32) - ustart[sorted_e]
    row_token = jnp.zeros((n_rows,), jnp.int32).at[dest_sorted].set(order // 2)
    pos = jnp.zeros((n_assign,), jnp.int32).at[order].set(dest_sorted)
    tile_start = jnp.arange(n_rows // tm, dtype=jnp.int32) * tm
    tile_expert = jnp.minimum(jnp.searchsorted(pend, tile_start, side="right"),
                              N_EXPERTS - 1).astype(jnp.int32)
    nused = (pend[-1:] // tm).astype(jnp.int32)
    return tile_expert, row_token, nused, pos


def _to_heads(x, batch, seq):
    return x.reshape(batch, seq, HEADS, HEAD_DIM).transpose(0, 2, 1, 3).reshape(
        batch * HEADS, seq, HEAD_DIM)


def _from_heads(x, batch, seq):
    return x.reshape(batch, HEADS, seq, HEAD_DIM).transpose(0, 2, 1, 3).reshape(
        batch * seq, WIDTH)


def _pad_cols(w, width):
    return jnp.pad(w, ((0, 0), (0, width - w.shape[1])))


def _pad_rows(w, rows):
    return jnp.pad(w, ((0, rows - w.shape[0]), (0, 0)))


def kernel(x, norm_mix_g, norm_ffn_g, w_in, mu_rkv, mu_x, w0, w1, w2, a0, a1, a2, g1, g2, k_k, k_a, r_k, lnx_g, lnx_b, vres_mu, vres_v0, vres_v1, vres_v2, q_gain, k_gain, rel_bias, w_br_rwkv, w_br_moba, w_out, w_grp, b_grp, w_exp, b_exp, w_gate, w_up, w_down):
    batch, seq, d_model = x.shape
    depth = w_in.shape[0]
    t = batch * seq
    nb = seq // MOBA_BLOCK
    assert seq % MOBA_BLOCK == 0 and seq % SCAN_BLOCK == 0 and (batch * HEADS) % SCAN_GROUP == 0

    head_of = jnp.arange(WIDTH) // HEAD_DIM
    ones_bd = (head_of[:, None] == head_of[None, :]).astype(BF16)
    bias_tiles = _rel_bias_tiles(rel_bias)
    gains = jnp.stack([jnp.tile(q_gain, (1, HEADS)), jnp.tile(k_gain, (1, HEADS))], axis=1)

    xf = x.reshape(t, d_model)
    v_first = None
    for l in range(depth):
        has_vres = l > 0
        lora_w = [w1[l], a1[l], g1[l]]
        lora_mu = [mu_x[l, 0], mu_x[l, 1], mu_x[l, 2]]
        if has_vres:
            lora_w.append(vres_v1[l - 1])
            lora_mu.append(vres_mu[l - 1])
        p1 = _pad_cols(jnp.concatenate([w * (1.0 - m)[:, None] for w, m in zip(lora_w, lora_mu)], 1),
                       LORA_PAD)
        p2 = _pad_cols(jnp.concatenate([w * m[:, None] for w, m in zip(lora_w, lora_mu)], 1), LORA_PAD)
        w_all = jnp.concatenate([w_in[l], p1, p2], axis=1).astype(BF16)
        wwa = jnp.concatenate([
            jnp.concatenate([w2[l], jnp.zeros((DECAY_LORA, WIDTH), F32)], 1),
            jnp.concatenate([jnp.zeros((AAA_LORA, WIDTH), F32), a2[l]], 1)], 0).astype(BF16)
        v2p = (_pad_rows(vres_v2[l - 1], 128) if has_vres else jnp.zeros((128, WIDTH), F32)).astype(BF16)
        v0 = vres_v0[l - 1] if has_vres else jnp.zeros((WIDTH,), F32)
        vecs = jnp.stack([w0[l], a0[l], k_k[l], k_a[l], r_k[l].reshape(-1), v0,
                          jnp.zeros((WIDTH,), F32), jnp.zeros((WIDTH,), F32)])

        proj = _norm_proj(xf, norm_mix_g[l], w_all)

        r, lw, k, v, kk, b, g, bonus = _rwkv_prep(proj, seq, mu_rkv[l], vecs, wwa,
                                                  g2[l].astype(BF16), v2p, ones_bd, v_first)
        if l == 0:
            v_first = v
        heads = [_to_heads(a_, batch, seq) for a_ in (r, lw, k, v, kk, b)]
        y_scan = _from_heads(_rwkv_scan(*heads), batch, seq)

        qf, kf, kmean = _moba_prep(proj, gains[l], ones_bd)
        km = kmean.reshape(batch, nb, HEADS, HEAD_DIM)
        km_mat = jnp.einsum("bjhd,hg->bhdgj", km, jnp.eye(HEADS, dtype=F32)).reshape(
            batch, WIDTH, HEADS * nb)
        sc = _gating_scores(qf, km_mat, batch)
        sc_h = sc.reshape(batch, seq, HEADS, nb).transpose(0, 2, 1, 3).reshape(batch * HEADS, seq, nb)
        q_h = _to_heads(qf.astype(BF16), batch, seq)
        k_h = _to_heads(kf.astype(BF16), batch, seq)
        v_h = _to_heads(proj[:, 5 * WIDTH:6 * WIDTH].astype(BF16), batch, seq)
        y_m = _from_heads(_moba_attn(q_h, k_h, v_h, sc_h, bias_tiles), batch, seq)

        wrt = _pad_cols(jnp.concatenate([w_grp[l], w_exp[l]], axis=1), 128)
        brt = _pad_cols(jnp.concatenate([b_grp[l], b_exp[l]])[None, :], 128)
        lnx = jnp.stack([lnx_g[l], lnx_b[l]])
        xf, xn2, route = _merge_router(
            xf, y_scan, bonus, g, y_m, proj, lnx, ones_bd,
            w_br_rwkv[l].astype(BF16), w_br_moba[l].astype(BF16), w_out[l].astype(BF16),
            norm_ffn_g[l].reshape(1, d_model), wrt, brt)

        tile_expert, row_token, nused, pos = _moe_plan(route, t)
        ys = _moe_ffn(tile_expert, row_token, nused, xn2,
                      w_gate[l].astype(BF16), w_up[l].astype(BF16), w_down[l].astype(BF16))
        xf = _moe_combine(pos, ys, xf, route)
    return xf.reshape(batch, seq, d_model)
```

```python
import functools
import math

import jax
import jax.numpy as jnp
from jax import lax
from jax.experimental import pallas as pl
from jax.experimental.pallas import tpu as pltpu

F32 = jnp.float32
BF16 = jnp.bfloat16

HEADS = 8
HEAD_DIM = 64
WIDTH = HEADS * HEAD_DIM
DECAY_LORA, AAA_LORA, MV_LORA, GATE_LORA = 64, 64, 32, 128
LN_X_EPS = 64e-5
MOBA_BLOCK = 256
MOBA_TOPK = 3
REL_BUCKETS = 32
REL_MAX_DIST = 128
N_GROUPS = 4
EXPERTS_PER_GROUP = 8
N_EXPERTS = N_GROUPS * EXPERTS_PER_GROUP
NORM_EPS = 1e-6
QK_EPS = 1e-6

CHUNK = 64
SCAN_BLOCK = 512
SCAN_GROUP = 8
LORA_PAD = 512
ROW_TILE = 256
MOE_TILE = 256
NEG_BIG = -1e30
LOG2E = math.log2(math.e)
ATTN_HEADS_PER_STEP = 2
VMEM_LIMIT = 48 * 1024 * 1024


def _cparams(*sem):
    return pltpu.CompilerParams(dimension_semantics=sem, vmem_limit_bytes=VMEM_LIMIT)


def _dot(a, b):
    return jnp.dot(a.astype(BF16), b.astype(BF16), preferred_element_type=F32)


def _dot_nt(a, b):
    return lax.dot_general(a.astype(BF16), b.astype(BF16), (((1,), (1,)), ((), ())),
                           preferred_element_type=F32)


def _dot_tn(a, b):
    return lax.dot_general(a.astype(BF16), b.astype(BF16), (((0,), (0,)), ((), ())),
                           preferred_element_type=F32)


def _split(x):
    hi = x.astype(BF16)
    lo = (x - hi.astype(F32)).astype(BF16)
    return hi, lo


def _dot_lsplit(a, b_exact):
    hi, lo = _split(a)
    return (jnp.dot(hi, b_exact, preferred_element_type=F32)
            + jnp.dot(lo, b_exact, preferred_element_type=F32))


def _dot_rsplit(a_exact, b):
    hi, lo = _split(b)
    return (jnp.dot(a_exact, hi, preferred_element_type=F32)
            + jnp.dot(a_exact, lo, preferred_element_type=F32))


def _dot_x3(a, b):
    ah, al = _split(a)
    bh, bl = _split(b)
    return (jnp.dot(ah, bh, preferred_element_type=F32)
            + jnp.dot(al, bh, preferred_element_type=F32)
            + jnp.dot(ah, bl, preferred_element_type=F32))


def _sigmoid(x):
    return 1.0 / (1.0 + jnp.exp(-x))


def _norm_proj_kernel(x_ref, g_ref, w_ref, o_ref, xn_ref):
    @pl.when(pl.program_id(1) == 0)
    def _():
        x = x_ref[...]
        ms = jnp.mean(x * x, axis=-1, keepdims=True)
        xn_ref[...] = (x * lax.rsqrt(ms + NORM_EPS) * g_ref[...]).astype(BF16)

    o_ref[...] = jnp.dot(xn_ref[...], w_ref[...], preferred_element_type=F32)


def _norm_proj(x, g, w, tm=512, tn=1024):
    m, k = x.shape
    n = w.shape[1]
    return pl.pallas_call(
        _norm_proj_kernel,
        grid=(m // tm, n // tn),
        in_specs=[pl.BlockSpec((tm, k), lambda i, j: (i, 0)),
                  pl.BlockSpec((1, k), lambda i, j: (0, 0)),
                  pl.BlockSpec((k, tn), lambda i, j: (0, j))],
        out_specs=pl.BlockSpec((tm, tn), lambda i, j: (i, j)),
        out_shape=jax.ShapeDtypeStruct((m, n), F32),
        scratch_shapes=[pltpu.VMEM((tm, k), BF16)],
        compiler_params=_cparams("parallel", "arbitrary"),
    )(x, g.reshape(1, k), w)


def _shift_rows(x, halo_ref, first):
    prev = jnp.where(first, 0.0, halo_ref[7:8, :])
    rolled = pltpu.roll(x, 1, axis=0)
    row = lax.broadcasted_iota(jnp.int32, x.shape, 0)
    return jnp.where(row == 0, prev, rolled)


def _rwkv_prep_kernel(has_vres, tiles_per_seq, *refs):
    (rp_ref, kp_ref, vp_ref, p1_ref, p2_ref, rph_ref, kph_ref, vph_ref, p2h_ref,
     mu_ref, vec_ref, wwa_ref, g2_ref, v2_ref, ones_ref) = refs[:15]
    refs = refs[15:]
    if has_vres:
        vfirst_ref = refs[0]
        refs = refs[1:]
    r_o, lw_o, k_o, v_o, kk_o, b_o, g_o, bonus_o = refs

    first = (pl.program_id(0) % tiles_per_seq) == 0
    rp, kp, vp = rp_ref[...], kp_ref[...], vp_ref[...]
    r = rp + (_shift_rows(rp, rph_ref, first) - rp) * mu_ref[0:1, :]
    k = kp + (_shift_rows(kp, kph_ref, first) - kp) * mu_ref[1:2, :]
    v = vp + (_shift_rows(vp, vph_ref, first) - vp) * mu_ref[2:3, :]
    lo = p1_ref[...] + _shift_rows(p2_ref[...], p2h_ref, first)

    w0, a0, k_k, k_a = vec_ref[0:1, :], vec_ref[1:2, :], vec_ref[2:3, :], vec_ref[3:4, :]
    r_k, v0 = vec_ref[4:5, :], vec_ref[5:6, :]

    h_wa = lo[:, 0:128]
    lane = lax.broadcasted_iota(jnp.int32, h_wa.shape, 1)
    h_wa = jnp.where(lane < DECAY_LORA, jnp.tanh(h_wa), h_wa)
    wa = _dot(h_wa, wwa_ref[...])
    z = -(w0 + wa[:, :WIDTH])
    softplus = jnp.maximum(z, 0.0) + jnp.log(1.0 + jnp.exp(-jnp.abs(z)))
    logw = -jnp.exp(-softplus - 0.5)
    a = _sigmoid(a0 + wa[:, WIDTH:])
    g = _dot(_sigmoid(lo[:, 128:256]), g2_ref[...])
    if has_vres:
        gate_v = _sigmoid(v0 + _dot(lo[:, 256:384], v2_ref[...]))
        v = v + (vfirst_ref[...] - v) * gate_v

    ones_bd = ones_ref[...]
    kk = k * k_k
    ss = _dot_lsplit(kk * kk, ones_bd)
    kk = kk / jnp.maximum(jnp.sqrt(ss), 1e-12)
    k = k * (1.0 + (a - 1.0) * k_a)
    bonus = _dot_lsplit(r * k * r_k, ones_bd) * v

    r_o[...] = r
    lw_o[...] = logw
    k_o[...] = k
    v_o[...] = v
    kk_o[...] = kk
    b_o[...] = kk * a
    g_o[...] = g
    bonus_o[...] = bonus


def _rwkv_prep(proj, seq, mu_rkv, vecs, wwa, g2, v2p, ones_bd, v_first):
    t = proj.shape[0]
    ts = ROW_TILE
    has_vres = v_first is not None
    col = lambda c: pl.BlockSpec((ts, WIDTH), lambda i, c=c: (i, c))
    halo = lambda c: pl.BlockSpec((8, WIDTH), lambda i, c=c: (jnp.maximum(i * (ts // 8) - 1, 0), c))
    full = lambda shape: pl.BlockSpec(shape, lambda i: tuple(0 for _ in shape))
    p1c = 5120 // WIDTH
    in_specs = [col(0), col(1), col(2), col(p1c), col(p1c + 1),
                halo(0), halo(1), halo(2), halo(p1c + 1),
                full((3, WIDTH)), full((8, WIDTH)), full((128, 2 * WIDTH)),
                full((128, WIDTH)), full((128, WIDTH)), full((WIDTH, WIDTH))]
    args = [proj] * 9 + [mu_rkv, vecs, wwa, g2, v2p, ones_bd]
    if has_vres:
        in_specs.append(pl.BlockSpec((ts, WIDTH), lambda i: (i, 0)))
        args.append(v_first)
    out = pl.pallas_call(
        functools.partial(_rwkv_prep_kernel, has_vres, seq // ts),
        grid=(t // ts,),
        in_specs=in_specs,
        out_specs=[pl.BlockSpec((ts, WIDTH), lambda i: (i, 0))] * 8,
        out_shape=[jax.ShapeDtypeStruct((t, WIDTH), F32)] * 8,
        compiler_params=_cparams("parallel"),
    )(*args)
    return out


def _rwkv_chunk_prep_kernel(r_ref, lw_ref, k_ref, v_ref, kk_ref, b_ref,
                            atp_ref, rt_ref, arb_ref, bh_ref, w2_ref, y2_ref, z2_ref, gc_ref):
    c_len = CHUNK
    n_chunks = r_ref.shape[1] // c_len
    row = lax.broadcasted_iota(jnp.int32, (c_len, c_len), 0)
    col = lax.broadcasted_iota(jnp.int32, (c_len, c_len), 1)
    strict = row > col
    incl = row >= col
    tril = incl.astype(BF16)
    m16 = (row // 16) == (col // 16)
    m32 = (row // 32) == (col // 32)
    eye = (row == col).astype(F32)

    chunks = range(n_chunks)
    sls = [pl.ds(c * c_len, c_len) for c in chunks]
    each = lambda fn, *cols: [fn(*args) for args in zip(*cols)]
    lw = [lw_ref[0, sl, :] for sl in sls]
    v = [v_ref[0, sl, :] for sl in sls]
    cum = each(lambda x: _dot_rsplit(tril, x), lw)
    e_pos = each(jnp.exp, cum)
    e_neg = each(lambda x: jnp.exp(-x), cum)
    g_last = [x[c_len - 1:c_len, :] for x in e_pos]
    at = each(lambda sl, x, y: -kk_ref[0, sl, :] * jnp.exp(x - y), sls, cum, lw)
    rt = each(lambda sl, x: r_ref[0, sl, :] * x, sls, e_pos)
    bt = each(lambda sl, x: b_ref[0, sl, :] * x, sls, e_neg)
    kt = each(lambda sl, x: k_ref[0, sl, :] * x, sls, e_neg)
    ar = each(lambda x, y: jnp.concatenate([x, y], axis=0).astype(BF16), at, rt)
    mb = each(_dot_nt, ar, bt)
    mk = each(_dot_nt, ar, kt)
    aab = [jnp.where(strict, x[:c_len], 0.0) for x in mb]
    arb = [jnp.where(incl, x[c_len:], 0.0) for x in mb]
    aak = [jnp.where(strict, x[:c_len], 0.0) for x in mk]
    ark = [jnp.where(incl, x[c_len:], 0.0) for x in mk]
    d1 = [jnp.where(m16, x, 0.0) for x in aab]
    d2 = each(_dot, d1, d1)
    t1 = each(lambda x, y: _dot(eye + x, eye + y), d1, d2)
    d4 = each(_dot, d2, d2)
    t2 = each(lambda x, y: _dot(x, eye + y), t1, d4)
    d8 = each(_dot, d4, d4)
    t3 = each(lambda x, y: _dot(x, eye + y), t2, d8)
    x32 = each(lambda t, a: _dot(t, jnp.where(m32 & jnp.logical_not(m16), a, 0.0)), t3, aab)
    t32 = each(lambda t, x: t + _dot(x, t), t3, x32)
    x64 = each(lambda t, a: _dot(t, jnp.where(m32, 0.0, a)), t32, aab)
    tinv = each(lambda t, x: t + _dot(x, t), t32, x64)
    akv = each(_dot, aak, v)
    atp = each(_dot, tinv, at)
    w2 = each(_dot, tinv, akv)
    y2 = each(_dot, ark, v)
    z2 = each(lambda x, y, g: _dot_tn(x, y * g), v, kt, g_last)
    for c in chunks:
        sl = sls[c]
        atp_ref[0, sl, :] = atp[c].astype(BF16)
        rt_ref[0, sl, :] = rt[c].astype(BF16)
        arb_ref[0, sl, :] = arb[c].astype(BF16)
        bh_ref[0, sl, :] = (bt[c] * g_last[c]).astype(BF16)
        w2_ref[0, sl, :] = w2[c]
        y2_ref[0, sl, :] = y2[c]
        z2_ref[0, sl, :] = z2[c]
        gc_ref[0, c:c + 1, :] = g_last[c]


def _rwkv_scan_kernel(atp_ref, rt_ref, arb_ref, bh_ref, w2_ref, y2_ref, z2_ref, gc_ref,
                      y_ref, s_ref):
    c_len = CHUNK
    n_chunks = atp_ref.shape[1] // c_len
    group = atp_ref.shape[0]

    @pl.when(pl.program_id(1) == 0)
    def _():
        s_ref[...] = jnp.zeros_like(s_ref)

    def body(c, states):
        sl = pl.ds(pl.multiple_of(c * c_len, c_len), c_len)
        heads = range(group)
        sb = [states[gi].astype(BF16) for gi in heads]
        u = [_dot_nt(atp_ref[gi, sl, :], sb[gi]) + w2_ref[gi, sl, :] for gi in heads]
        ys = [_dot_nt(rt_ref[gi, sl, :], sb[gi]) + y2_ref[gi, sl, :] for gi in heads]
        ub = [x.astype(BF16) for x in u]
        upd = [_dot_tn(ub[gi], bh_ref[gi, sl, :]) for gi in heads]
        for gi in heads:
            y_ref[gi, sl, :] = ys[gi] + _dot(arb_ref[gi, sl, :], ub[gi])
        return tuple(states[gi] * gc_ref[gi, pl.ds(c, 1), :] + upd[gi] + z2_ref[gi, sl, :]
                     for gi in heads)

    states = lax.fori_loop(0, n_chunks, body, tuple(s_ref[gi] for gi in range(group)))
    for gi in range(group):
        s_ref[gi] = states[gi]


def _rwkv_scan(r, lw, k, v, kk, b):
    bh, s, n = r.shape
    sb = min(SCAN_BLOCK, s)
    nc = sb // CHUNK
    blk = pl.BlockSpec((1, sb, n), lambda i, j: (i, j, 0))
    gblk = pl.BlockSpec((1, nc, n), lambda i, j: (i, j, 0))
    outs = pl.pallas_call(
        _rwkv_chunk_prep_kernel,
        grid=(bh, s // sb),
        in_specs=[blk] * 6,
        out_specs=[blk] * 7 + [gblk],
        out_shape=[jax.ShapeDtypeStruct((bh, s, n), BF16)] * 4
        + [jax.ShapeDtypeStruct((bh, s, n), F32)] * 3
        + [jax.ShapeDtypeStruct((bh, s // CHUNK, n), F32)],
        compiler_params=_cparams("parallel", "parallel"),
    )(r, lw, k, v, kk, b)
    grp = SCAN_GROUP
    blk = pl.BlockSpec((grp, sb, n), lambda i, j: (i, j, 0))
    gblk = pl.BlockSpec((grp, nc, n), lambda i, j: (i, j, 0))
    return pl.pallas_call(
        _rwkv_scan_kernel,
        grid=(bh // grp, s // sb),
        in_specs=[blk] * 7 + [gblk],
        out_specs=blk,
        out_shape=jax.ShapeDtypeStruct((bh, s, n), F32),
        scratch_shapes=[pltpu.VMEM((grp, n, n), F32)],
        compiler_params=_cparams("parallel", "arbitrary"),
    )(*outs)


def _moba_prep_kernel(q_ref, k_ref, gain_ref, ones_ref, qo_ref, ko_ref, km_ref):
    ones_bd = ones_ref[...]
    q = q_ref[...]
    k = k_ref[...]
    inv_d = 1.0 / HEAD_DIM
    qn = q * lax.rsqrt(_dot_lsplit(q * q, ones_bd) * inv_d + QK_EPS) * gain_ref[0:1, :]
    kn = k * lax.rsqrt(_dot_lsplit(k * k, ones_bd) * inv_d + QK_EPS) * gain_ref[1:2, :]
    qo_ref[...] = qn * (HEAD_DIM ** -0.5 * LOG2E)
    ko_ref[...] = kn
    km_ref[0] = jnp.mean(kn, axis=0, keepdims=True)


def _moba_prep(proj, gains, ones_bd):
    t = proj.shape[0]
    ts = MOBA_BLOCK
    return pl.pallas_call(
        _moba_prep_kernel,
        grid=(t // ts,),
        in_specs=[pl.BlockSpec((ts, WIDTH), lambda i: (i, 3)),
                  pl.BlockSpec((ts, WIDTH), lambda i: (i, 4)),
                  pl.BlockSpec((2, WIDTH), lambda i: (0, 0)),
                  pl.BlockSpec((WIDTH, WIDTH), lambda i: (0, 0))],
        out_specs=[pl.BlockSpec((ts, WIDTH), lambda i: (i, 0)),
                   pl.BlockSpec((ts, WIDTH), lambda i: (i, 0)),
                   pl.BlockSpec((1, 1, WIDTH), lambda i: (i, 0, 0))],
        out_shape=[jax.ShapeDtypeStruct((t, WIDTH), F32),
                   jax.ShapeDtypeStruct((t, WIDTH), F32),
                   jax.ShapeDtypeStruct((t // ts, 1, WIDTH), F32)],
        compiler_params=_cparams("parallel"),
    )(proj, proj, gains, ones_bd)


def _gating_kernel(q_ref, km_ref, o_ref):
    ah, al = _split(km_ref[0])
    bh, bl = _split(q_ref[...])
    nt = lambda a, b: lax.dot_general(a, b, (((1,), (1,)), ((), ())), preferred_element_type=F32)
    o_ref[0] = nt(ah, bh) + nt(al, bh) + nt(ah, bl)


def _gating_scores(qf, km_t, batch):
    t = qf.shape[0]
    ts = ROW_TILE
    per = t // batch // ts
    rows = km_t.shape[1]
    return pl.pallas_call(
        _gating_kernel,
        grid=(batch, per),
        in_specs=[pl.BlockSpec((ts, WIDTH), lambda b, i: (b * per + i, 0)),
                  pl.BlockSpec((1, rows, WIDTH), lambda b, i: (b, 0, 0))],
        out_specs=pl.BlockSpec((1, rows, ts), lambda b, i: (b, 0, i)),
        out_shape=jax.ShapeDtypeStruct((batch, rows, t // batch), F32),
        compiler_params=_cparams("parallel", "parallel"),
    )(qf, km_t)


def _rel_bias_kernel(tab_ref, o_ref):
    h = pl.program_id(0)
    blk = MOBA_BLOCK
    key = lax.broadcasted_iota(jnp.int32, (blk, blk), 0)
    qry = lax.broadcasted_iota(jnp.int32, (blk, blk), 1)
    max_exact = REL_BUCKETS // 2

    def lookup(n):
        nf = jnp.maximum(n, 1).astype(F32)
        large = max_exact + (jnp.log(nf / max_exact) / math.log(REL_MAX_DIST / max_exact)
                             * (REL_BUCKETS - max_exact)).astype(jnp.int32)
        bucket = jnp.where(n < max_exact, n, jnp.minimum(large, REL_BUCKETS - 1))
        out = jnp.zeros((blk, blk), F32)
        for bkt in range(REL_BUCKETS):
            out = jnp.where(bucket == bkt, tab_ref[bkt, h], out)
        return out

    dist = qry - key
    far = lookup(dist + 2 * blk)
    o_ref[0, 0] = jnp.where(dist >= 0, (lookup(jnp.maximum(dist, 0)) - far) * LOG2E, NEG_BIG)
    o_ref[0, 1] = (lookup(dist + blk) - far) * LOG2E


def _rel_bias_tiles(rel_bias):
    return pl.pallas_call(
        _rel_bias_kernel,
        grid=(HEADS,),
        in_specs=[pl.BlockSpec(memory_space=pltpu.SMEM)],
        out_specs=pl.BlockSpec((1, 2, MOBA_BLOCK, MOBA_BLOCK), lambda h: (h, 0, 0, 0)),
        out_shape=jax.ShapeDtypeStruct((HEADS, 2, MOBA_BLOCK, MOBA_BLOCK), F32),
        compiler_params=_cparams("parallel"),
    )(rel_bias)


def _moba_attn_kernel(q_ref, k_ref, vt_ref, sc_ref, bias_ref, o_ref, mask_ref):
    blk = MOBA_BLOCK
    i = pl.program_id(1)
    nb = sc_ref.shape[1]
    n_heads = q_ref.shape[0]
    bidx = lax.broadcasted_iota(jnp.int32, (nb, blk), 0)
    past = bidx < i
    own = pl.ds(pl.multiple_of(i * blk, blk), blk)

    carries = []
    for hh in range(n_heads):
        scm = jnp.where(past, sc_ref[hh], -jnp.inf)
        sel = jnp.zeros((nb, blk), jnp.bool_)
        for _ in range(MOBA_TOPK):
            top = jnp.max(scm, axis=0, keepdims=True)
            idx = jnp.min(jnp.where(scm == top, bidx, nb), axis=0, keepdims=True)
            pick = bidx == idx
            sel = jnp.logical_or(sel, pick)
            scm = jnp.where(pick, -jnp.inf, scm)
        mask_ref[hh] = jnp.where(jnp.logical_and(sel, past), 0.0, NEG_BIG)

        s = _dot_nt(k_ref[hh, own, :], q_ref[hh]) + bias_ref[hh, 0]
        m = jnp.max(s, axis=0, keepdims=True)
        p = jnp.exp2(s - m)
        l = jnp.sum(p, axis=0, keepdims=True)
        acc = jnp.dot(vt_ref[hh, :, own], p.astype(BF16), preferred_element_type=F32)
        carries.append((m, l, acc))

    def scores(hh, j):
        sl = pl.ds(pl.multiple_of(j * blk, blk), blk)
        return _dot_nt(k_ref[hh, sl, :], q_ref[hh])

    def block_update(hh, j, s, carry):
        m, l, acc = carry
        sl = pl.ds(pl.multiple_of(j * blk, blk), blk)
        mrow = mask_ref[hh, pl.ds(j, 1), :]
        m_new = jnp.maximum(m, jnp.max(s, axis=0, keepdims=True) + mrow)
        alpha = jnp.exp2(m - m_new)
        p = jnp.exp2(s - (m_new - mrow))
        l = alpha * l + jnp.sum(p, axis=0, keepdims=True)
        acc = alpha * acc + jnp.dot(vt_ref[hh, :, sl], p.astype(BF16), preferred_element_type=F32)
        return m_new, l, acc

    def with_prev(carry):
        return tuple(block_update(hh, i - 1, scores(hh, i - 1) + bias_ref[hh, 1], carry[hh])
                     for hh in range(n_heads))

    carries = lax.cond(i > 0, with_prev, lambda c: c, tuple(carries))

    n_far = jnp.maximum(i - 1, 0)

    def far_body(j, carry):
        s_cur, stats = carry
        j_next = jnp.minimum(j + 1, jnp.maximum(n_far - 1, 0))
        s_next = tuple(scores(hh, j_next) for hh in range(n_heads))
        stats = tuple(block_update(hh, j, s_cur[hh], stats[hh]) for hh in range(n_heads))
        return s_next, stats

    s_first = tuple(scores(hh, 0) for hh in range(n_heads))
    _, carries = lax.fori_loop(0, n_far, far_body, (s_first, carries))
    for hh in range(n_heads):
        m, l, acc = carries[hh]
        o_ref[hh] = acc / l


def _moba_attn(q, k, vt, sc_t, bias):
    bh, s, dh = q.shape
    nb = sc_t.shape[1]
    blk = MOBA_BLOCK
    hp = ATTN_HEADS_PER_STEP
    return pl.pallas_call(
        _moba_attn_kernel,
        grid=(bh // hp, s // blk),
        in_specs=[pl.BlockSpec((hp, blk, dh), lambda b, i: (b, i, 0)),
                  pl.BlockSpec((hp, s, dh), lambda b, i: (b, 0, 0)),
                  pl.BlockSpec((hp, dh, s), lambda b, i: (b, 0, 0)),
                  pl.BlockSpec((hp, nb, blk), lambda b, i: (b, 0, i)),
                  pl.BlockSpec((hp, 2, blk, blk), lambda b, i: (b % (HEADS // hp), 0, 0, 0))],
        out_specs=pl.BlockSpec((hp, dh, blk), lambda b, i: (b, 0, i)),
        out_shape=jax.ShapeDtypeStruct((bh, dh, s), F32),
        scratch_shapes=[pltpu.VMEM((hp, nb, blk), F32)],
        compiler_params=_cparams("parallel", "parallel"),
    )(q, k, vt, sc_t, bias)


def _merge_router_kernel(x_ref, ys_ref, bonus_ref, g_ref, ym_ref, gr_ref, gm_ref,
                         lnx_ref, ones_ref, wr_ref, wm_ref, wo_ref, gf_ref, wrt_ref, brt_ref,
                         xo_ref, xn_ref, route_ref):
    ones_bd = ones_ref[...]
    inv_n = 1.0 / HEAD_DIM
    y = ys_ref[...]
    mean = _dot_lsplit(y, ones_bd) * inv_n
    d = y - mean
    var = _dot_lsplit(d * d, ones_bd) * inv_n
    yr = d * lax.rsqrt(var + LN_X_EPS) * lnx_ref[0:1, :] + lnx_ref[1:2, :]
    yr = (yr + bonus_ref[...]) * g_ref[...]
    h = (_sigmoid(gr_ref[...]) * _dot(yr, wr_ref[...])
         + _sigmoid(gm_ref[...]) * _dot(ym_ref[...], wm_ref[...]))
    x = x_ref[...] + _dot(h, wo_ref[...])
    xo_ref[...] = x

    ms = jnp.mean(x * x, axis=-1, keepdims=True)
    xn = x * lax.rsqrt(ms + NORM_EPS) * gf_ref[...]
    xn_ref[...] = xn

    logits = _dot_x3(xn, wrt_ref[...]) + brt_ref[...]
    lane = lax.broadcasted_iota(jnp.int32, logits.shape, 1)
    big = logits.shape[1]
    is_grp = lane < N_GROUPS
    lg = jnp.where(is_grp, logits, -jnp.inf)
    mg = jnp.max(lg, axis=1, keepdims=True)
    g_sel = jnp.min(jnp.where(lg == mg, lane, big), axis=1, keepdims=True)
    p_top = 1.0 / jnp.sum(jnp.where(is_grp, jnp.exp(logits - mg), 0.0), axis=1, keepdims=True)
    lo_lane = N_GROUPS + g_sel * EXPERTS_PER_GROUP
    in_grp = jnp.logical_and(lane >= lo_lane, lane < lo_lane + EXPERTS_PER_GROUP)
    le = jnp.where(in_grp, logits, -jnp.inf)
    m1 = jnp.max(le, axis=1, keepdims=True)
    i1 = jnp.min(jnp.where(le == m1, lane, big), axis=1, keepdims=True)
    le2 = jnp.where(lane == i1, -jnp.inf, le)
    m2 = jnp.max(le2, axis=1, keepdims=True)
    i2 = jnp.min(jnp.where(le2 == m2, lane, big), axis=1, keepdims=True)
    e2 = jnp.exp(m2 - m1)
    gate1 = p_top / (1.0 + e2)
    gate2 = p_top * e2 / (1.0 + e2)
    route = jnp.where(lane == 0, (i1 - N_GROUPS).astype(F32),
                      jnp.where(lane == 1, (i2 - N_GROUPS).astype(F32),
                                jnp.where(lane == 2, gate1, jnp.where(lane == 3, gate2, 0.0))))
    route_ref[...] = route


def _merge_router(x, ys, bonus, g, ym, proj, lnx, ones_bd, wr, wm, wo, gf, wrt, brt):
    t, d = x.shape
    tm = ROW_TILE
    row = lambda w: pl.BlockSpec((tm, w), lambda i: (i, 0))
    full = lambda shape: pl.BlockSpec(shape, lambda i: tuple(0 for _ in shape))
    return pl.pallas_call(
        _merge_router_kernel,
        grid=(t // tm,),
        in_specs=[row(d), row(WIDTH), row(WIDTH), row(WIDTH), row(WIDTH),
                  pl.BlockSpec((tm, d), lambda i: (i, 3)),
                  pl.BlockSpec((tm, d), lambda i: (i, 4)),
                  full((2, WIDTH)), full((WIDTH, WIDTH)), full((WIDTH, d)), full((WIDTH, d)),
                  full((d, d)), full((1, d)), full((d, 128)), full((1, 128))],
        out_specs=[row(d), row(d), row(128)],
        out_shape=[jax.ShapeDtypeStruct((t, d), F32), jax.ShapeDtypeStruct((t, d), F32),
                   jax.ShapeDtypeStruct((t, 128), F32)],
        compiler_params=_cparams("parallel"),
    )(x, ys, bonus, g, ym, proj, proj, lnx, ones_bd, wr, wm, wo, gf, wrt, brt)


def _moe_ffn_kernel(texp_ref, rtok_ref, nused_ref, x_hbm, wg_ref, wu_ref, wd_ref, o_ref,
                    xbuf, sem):
    i = pl.program_id(0)
    tm = xbuf.shape[0]

    def row_copy(r):
        tok = rtok_ref[i * tm + r]
        return pltpu.make_async_copy(x_hbm.at[pl.ds(tok, 1), :], xbuf.at[pl.ds(r, 1), :], sem)

    @pl.when(i < nused_ref[0])
    def _():
        def start(r, c):
            row_copy(r).start()
            return c

        def wait(r, c):
            row_copy(r).wait()
            return c

        lax.fori_loop(0, tm, start, 0)
        lax.fori_loop(0, tm, wait, 0)
        xb = xbuf[...].astype(BF16)
        hg = jnp.dot(xb, wg_ref[0], preferred_element_type=F32)
        hu = jnp.dot(xb, wu_ref[0], preferred_element_type=F32)
        h = hg * _sigmoid(hg) * hu
        o_ref[...] = jnp.dot(h.astype(BF16), wd_ref[0], preferred_element_type=F32)

    @pl.when(i >= nused_ref[0])
    def _():
        o_ref[...] = jnp.zeros_like(o_ref)


def _moe_ffn(tile_expert, row_token, nused, xn, wg, wu, wd):
    n_rows = row_token.shape[0]
    tm = MOE_TILE
    d = xn.shape[1]
    dh = wg.shape[2]
    grid_spec = pltpu.PrefetchScalarGridSpec(
        num_scalar_prefetch=3,
        grid=(n_rows // tm,),
        in_specs=[pl.BlockSpec(memory_space=pl.ANY),
                  pl.BlockSpec((1, d, dh), lambda i, te, rt, nu: (te[i], 0, 0)),
                  pl.BlockSpec((1, d, dh), lambda i, te, rt, nu: (te[i], 0, 0)),
                  pl.BlockSpec((1, dh, d), lambda i, te, rt, nu: (te[i], 0, 0))],
        out_specs=pl.BlockSpec((tm, d), lambda i, te, rt, nu: (i, 0)),
        scratch_shapes=[pltpu.VMEM((tm, d), F32), pltpu.SemaphoreType.DMA(())],
    )
    return pl.pallas_call(
        _moe_ffn_kernel,
        grid_spec=grid_spec,
        out_shape=jax.ShapeDtypeStruct((n_rows, d), F32),
        compiler_params=_cparams("arbitrary"),
    )(tile_expert, row_token, nused, xn, wg, wu, wd)


def _moe_combine_kernel(pos_ref, ys_hbm, x_ref, route_ref, o_ref, buf0, buf1, sem):
    i = pl.program_id(0)
    tm = buf0.shape[0]

    def row_copy(r, j, buf):
        p = pos_ref[(i * tm + r) * 2 + j]
        return pltpu.make_async_copy(ys_hbm.at[pl.ds(p, 1), :], buf.at[pl.ds(r, 1), :], sem)

    def start(r, c):
        row_copy(r, 0, buf0).start()
        row_copy(r, 1, buf1).start()
        return c

    def wait(r, c):
        row_copy(r, 0, buf0).wait()
        row_copy(r, 1, buf1).wait()
        return c

    lax.fori_loop(0, tm, start, 0)
    lax.fori_loop(0, tm, wait, 0)
    route = route_ref[...]
    o_ref[...] = x_ref[...] + route[:, 2:3] * buf0[...] + route[:, 3:4] * buf1[...]


def _moe_combine(pos, ys, x, route):
    t, d = x.shape
    tm = ROW_TILE
    grid_spec = pltpu.PrefetchScalarGridSpec(
        num_scalar_prefetch=1,
        grid=(t // tm,),
        in_specs=[pl.BlockSpec(memory_space=pl.ANY),
                  pl.BlockSpec((tm, d), lambda i, p: (i, 0)),
                  pl.BlockSpec((tm, 128), lambda i, p: (i, 0))],
        out_specs=pl.BlockSpec((tm, d), lambda i, p: (i, 0)),
        scratch_shapes=[pltpu.VMEM((tm, d), F32), pltpu.VMEM((tm, d), F32),
                        pltpu.SemaphoreType.DMA(())],
    )
    return pl.pallas_call(
        _moe_combine_kernel,
        grid_spec=grid_spec,
        out_shape=jax.ShapeDtypeStruct((t, d), F32),
        compiler_params=_cparams("arbitrary"),
    )(pos, ys, x, route)


def _moe_plan(route, n_tokens):
    tm = MOE_TILE
    e_flat = route[:, 0:2].astype(jnp.int32).reshape(-1)
    n_assign = e_flat.shape[0]
    n_rows = n_assign + N_EXPERTS * tm
    order = jnp.argsort(e_flat, stable=True).astype(jnp.int32)
    counts = jnp.zeros((N_EXPERTS,), jnp.int32).at[e_flat].add(1)
    padded = ((counts + tm - 1) // tm) * tm
    pend = jnp.cumsum(padded)
    pstart = pend - padded
    ustart = jnp.cumsum(counts) - counts
    sorted_e = e_flat[order]
    dest_sorted = pstart[sorted_e] + jnp.arange(n_assign, dtype=jnp.int32) - ustart[sorted_e]
    row_token = jnp.zeros((n_rows,), jnp.int32).at[dest_sorted].set(order // 2)
    pos = jnp.zeros((n_assign,), jnp.int32).at[order].set(dest_sorted)
    tile_start = jnp.arange(n_rows // tm, dtype=jnp.int32) * tm
    tile_expert = jnp.minimum(jnp.searchsorted(pend, tile_start, side="right"),
                              N_EXPERTS - 1).astype(jnp.int32)
    nused = (pend[-1:] // tm).astype(jnp.int32)
    return tile_expert, row_token, nused, pos


def _to_heads(x, batch, seq):
    return x.reshape(batch, seq, HEADS, HEAD_DIM).transpose(0, 2, 1, 3).reshape(
        batch * HEADS, seq, HEAD_DIM)


def _from_heads(x, batch, seq):
    return x.reshape(batch, HEADS, seq, HEAD_DIM).transpose(0, 2, 1, 3).reshape(
        batch * seq, WIDTH)


def _pad_cols(w, width):
    return jnp.pad(w, ((0, 0), (0, width - w.shape[1])))


def _pad_rows(w, rows):
    return jnp.pad(w, ((0, rows - w.shape[0]), (0, 0)))


def kernel(x, norm_mix_g, norm_ffn_g, w_in, mu_rkv, mu_x, w0, w1, w2, a0, a1, a2, g1, g2, k_k, k_a, r_k, lnx_g, lnx_b, vres_mu, vres_v0, vres_v1, vres_v2, q_gain, k_gain, rel_bias, w_br_rwkv, w_br_moba, w_out, w_grp, b_grp, w_exp, b_exp, w_gate, w_up, w_down):
    batch, seq, d_model = x.shape
    depth = w_in.shape[0]
    t = batch * seq
    nb = seq // MOBA_BLOCK
    assert seq % MOBA_BLOCK == 0 and seq % SCAN_BLOCK == 0 and (batch * HEADS) % SCAN_GROUP == 0

    head_of = jnp.arange(WIDTH) // HEAD_DIM
    ones_bd = (head_of[:, None] == head_of[None, :]).astype(BF16)
    bias_tiles = _rel_bias_tiles(rel_bias)
    gains = jnp.stack([jnp.tile(q_gain, (1, HEADS)), jnp.tile(k_gain, (1, HEADS))], axis=1)

    xf = x.reshape(t, d_model)
    v_first = None
    for l in range(depth):
        has_vres = l > 0
        lora_w = [w1[l], a1[l], g1[l]]
        lora_mu = [mu_x[l, 0], mu_x[l, 1], mu_x[l, 2]]
        if has_vres:
            lora_w.append(vres_v1[l - 1])
            lora_mu.append(vres_mu[l - 1])
        p1 = _pad_cols(jnp.concatenate([w * (1.0 - m)[:, None] for w, m in zip(lora_w, lora_mu)], 1),
                       LORA_PAD)
        p2 = _pad_cols(jnp.concatenate([w * m[:, None] for w, m in zip(lora_w, lora_mu)], 1), LORA_PAD)
        w_all = jnp.concatenate([w_in[l], p1, p2], axis=1).astype(BF16)
        wwa = jnp.concatenate([
            jnp.concatenate([w2[l], jnp.zeros((DECAY_LORA, WIDTH), F32)], 1),
            jnp.concatenate([jnp.zeros((AAA_LORA, WIDTH), F32), a2[l]], 1)], 0).astype(BF16)
        v2p = (_pad_rows(vres_v2[l - 1], 128) if has_vres else jnp.zeros((128, WIDTH), F32)).astype(BF16)
        v0 = vres_v0[l - 1] if has_vres else jnp.zeros((WIDTH,), F32)
        vecs = jnp.stack([w0[l], a0[l], k_k[l], k_a[l], r_k[l].reshape(-1), v0,
                          jnp.zeros((WIDTH,), F32), jnp.zeros((WIDTH,), F32)])

        proj = _norm_proj(xf, norm_mix_g[l], w_all)

        r, lw, k, v, kk, b, g, bonus = _rwkv_prep(proj, seq, mu_rkv[l], vecs, wwa,
                                                  g2[l].astype(BF16), v2p, ones_bd, v_first)
        if l == 0:
            v_first = v
        heads = [_to_heads(a_, batch, seq) for a_ in (r, lw, k, v, kk, b)]
        y_scan = _from_heads(_rwkv_scan(*heads), batch, seq)

        qf, kf, kmean = _moba_prep(proj, gains[l], ones_bd)
        km = kmean.reshape(batch, nb, HEADS, HEAD_DIM)
        km_t = jnp.einsum("bjhd,hg->bgjhd", km, jnp.eye(HEADS, dtype=F32)).reshape(
            batch, HEADS * nb, WIDTH)
        sc_t = _gating_scores(qf, km_t, batch).reshape(batch * HEADS, nb, seq)
        q_h = _to_heads(qf.astype(BF16), batch, seq)
        k_h = _to_heads(kf.astype(BF16), batch, seq)
        v_t = proj[:, 5 * WIDTH:6 * WIDTH].astype(BF16).reshape(batch, seq, HEADS, HEAD_DIM)
        v_t = v_t.transpose(0, 2, 3, 1).reshape(batch * HEADS, HEAD_DIM, seq)
        y_t = _moba_attn(q_h, k_h, v_t, sc_t, bias_tiles)
        y_m = y_t.reshape(batch, HEADS, HEAD_DIM, seq).transpose(0, 3, 1, 2).reshape(t, WIDTH)

        wrt = _pad_cols(jnp.concatenate([w_grp[l], w_exp[l]], axis=1), 128)
        brt = _pad_cols(jnp.concatenate([b_grp[l], b_exp[l]])[None, :], 128)
        lnx = jnp.stack([lnx_g[l], lnx_b[l]])
        xf, xn2, route = _merge_router(
            xf, y_scan, bonus, g, y_m, proj, lnx, ones_bd,
            w_br_rwkv[l].astype(BF16), w_br_moba[l].astype(BF16), w_out[l].astype(BF16),
            norm_ffn_g[l].reshape(1, d_model), wrt, brt)

        tile_expert, row_token, nused, pos = _moe_plan(route, t)
        ys = _moe_ffn(tile_expert, row_token, nused, xn2,
                      w_gate[l].astype(BF16), w_up[l].astype(BF16), w_down[l].astype(BF16))
        xf = _moe_combine(pos, ys, xf, route)
    return xf.reshape(batch, seq, d_model)
```

```python
import functools
import math

import jax
import jax.numpy as jnp
from jax import lax
from jax.experimental import pallas as pl
from jax.experimental.pallas import tpu as pltpu

F32 = jnp.float32
BF16 = jnp.bfloat16

HEADS = 8
HEAD_DIM = 64
WIDTH = HEADS * HEAD_DIM
DECAY_LORA, AAA_LORA, MV_LORA, GATE_LORA = 64, 64, 32, 128
LN_X_EPS = 64e-5
MOBA_BLOCK = 256
MOBA_TOPK = 3
REL_BUCKETS = 32
REL_MAX_DIST = 128
N_GROUPS = 4
EXPERTS_PER_GROUP = 8
N_EXPERTS = N_GROUPS * EXPERTS_PER_GROUP
NORM_EPS = 1e-6
QK_EPS = 1e-6

CHUNK = 64
SCAN_BLOCK = 512
SCAN_GROUP = 8
LORA_PAD = 512
ROW_TILE = 256
MOE_TILE = 256
NEG_BIG = -1e30
LOG2E = math.log2(math.e)
ATTN_HEADS_PER_STEP = 2
VMEM_LIMIT = 48 * 1024 * 1024


def _cparams(*sem):
    return pltpu.CompilerParams(dimension_semantics=sem, vmem_limit_bytes=VMEM_LIMIT)


def _dot(a, b):
    return jnp.dot(a.astype(BF16), b.astype(BF16), preferred_element_type=F32)


def _dot_nt(a, b):
    return lax.dot_general(a.astype(BF16), b.astype(BF16), (((1,), (1,)), ((), ())),
                           preferred_element_type=F32)


def _dot_tn(a, b):
    return lax.dot_general(a.astype(BF16), b.astype(BF16), (((0,), (0,)), ((), ())),
                           preferred_element_type=F32)


def _split(x):
    hi = x.astype(BF16)
    lo = (x - hi.astype(F32)).astype(BF16)
    return hi, lo


def _dot_lsplit(a, b_exact):
    hi, lo = _split(a)
    return (jnp.dot(hi, b_exact, preferred_element_type=F32)
            + jnp.dot(lo, b_exact, preferred_element_type=F32))


def _dot_rsplit(a_exact, b):
    hi, lo = _split(b)
    return (jnp.dot(a_exact, hi, preferred_element_type=F32)
            + jnp.dot(a_exact, lo, preferred_element_type=F32))


def _dot_x3(a, b):
    ah, al = _split(a)
    bh, bl = _split(b)
    return (jnp.dot(ah, bh, preferred_element_type=F32)
            + jnp.dot(al, bh, preferred_element_type=F32)
            + jnp.dot(ah, bl, preferred_element_type=F32))


def _sigmoid(x):
    return 1.0 / (1.0 + jnp.exp(-x))


def _norm_proj_kernel(x_ref, g_ref, w_ref, o_ref, xn_ref):
    @pl.when(pl.program_id(1) == 0)
    def _():
        x = x_ref[...]
        ms = jnp.mean(x * x, axis=-1, keepdims=True)
        xn_ref[...] = (x * lax.rsqrt(ms + NORM_EPS) * g_ref[...]).astype(BF16)

    o_ref[...] = jnp.dot(xn_ref[...], w_ref[...], preferred_element_type=F32)


def _norm_proj(x, g, w, tm=512, tn=1024):
    m, k = x.shape
    n = w.shape[1]
    return pl.pallas_call(
        _norm_proj_kernel,
        grid=(m // tm, n // tn),
        in_specs=[pl.BlockSpec((tm, k), lambda i, j: (i, 0)),
                  pl.BlockSpec((1, k), lambda i, j: (0, 0)),
                  pl.BlockSpec((k, tn), lambda i, j: (0, j))],
        out_specs=pl.BlockSpec((tm, tn), lambda i, j: (i, j)),
        out_shape=jax.ShapeDtypeStruct((m, n), F32),
        scratch_shapes=[pltpu.VMEM((tm, k), BF16)],
        compiler_params=_cparams("parallel", "arbitrary"),
    )(x, g.reshape(1, k), w)


def _store_heads(o_ref, val):
    for h in range(HEADS):
        o_ref[0, h] = val[:, h * HEAD_DIM:(h + 1) * HEAD_DIM].astype(o_ref.dtype)


def _load_heads(ref):
    return jnp.concatenate([ref[0, h] for h in range(HEADS)], axis=1)


def _shift_rows(x, halo_ref, first):
    prev = jnp.where(first, 0.0, halo_ref[7:8, :])
    rolled = pltpu.roll(x, 1, axis=0)
    row = lax.broadcasted_iota(jnp.int32, x.shape, 0)
    return jnp.where(row == 0, prev, rolled)


def _rwkv_prep_kernel(has_vres, tiles_per_seq, *refs):
    (rp_ref, kp_ref, vp_ref, p1_ref, p2_ref, rph_ref, kph_ref, vph_ref, p2h_ref,
     mu_ref, vec_ref, wwa_ref, g2_ref, v2_ref, ones_ref) = refs[:15]
    refs = refs[15:]
    if has_vres:
        vfirst_ref = refs[0]
        refs = refs[1:]
    r_o, lw_o, k_o, v_o, kk_o, b_o, vrow_o, g_o, bonus_o = refs

    first = (pl.program_id(0) % tiles_per_seq) == 0
    rp, kp, vp = rp_ref[...], kp_ref[...], vp_ref[...]
    r = rp + (_shift_rows(rp, rph_ref, first) - rp) * mu_ref[0:1, :]
    k = kp + (_shift_rows(kp, kph_ref, first) - kp) * mu_ref[1:2, :]
    v = vp + (_shift_rows(vp, vph_ref, first) - vp) * mu_ref[2:3, :]
    lo = p1_ref[...] + _shift_rows(p2_ref[...], p2h_ref, first)

    w0, a0, k_k, k_a = vec_ref[0:1, :], vec_ref[1:2, :], vec_ref[2:3, :], vec_ref[3:4, :]
    r_k, v0 = vec_ref[4:5, :], vec_ref[5:6, :]

    h_wa = lo[:, 0:128]
    lane = lax.broadcasted_iota(jnp.int32, h_wa.shape, 1)
    h_wa = jnp.where(lane < DECAY_LORA, jnp.tanh(h_wa), h_wa)
    wa = _dot(h_wa, wwa_ref[...])
    z = -(w0 + wa[:, :WIDTH])
    softplus = jnp.maximum(z, 0.0) + jnp.log(1.0 + jnp.exp(-jnp.abs(z)))
    logw = -jnp.exp(-softplus - 0.5)
    a = _sigmoid(a0 + wa[:, WIDTH:])
    g = _dot(_sigmoid(lo[:, 128:256]), g2_ref[...])
    if has_vres:
        gate_v = _sigmoid(v0 + _dot(lo[:, 256:384], v2_ref[...]))
        v = v + (vfirst_ref[...] - v) * gate_v

    ones_bd = ones_ref[...]
    kk = k * k_k
    ss = _dot_lsplit(kk * kk, ones_bd)
    kk = kk / jnp.maximum(jnp.sqrt(ss), 1e-12)
    k = k * (1.0 + (a - 1.0) * k_a)
    bonus = _dot_lsplit(r * k * r_k, ones_bd) * v

    _store_heads(r_o, r)
    _store_heads(lw_o, logw)
    _store_heads(k_o, k)
    _store_heads(v_o, v)
    _store_heads(kk_o, kk)
    _store_heads(b_o, kk * a)
    vrow_o[...] = v
    g_o[...] = g
    bonus_o[...] = bonus


def _rwkv_prep(proj, seq, mu_rkv, vecs, wwa, g2, v2p, ones_bd, v_first):
    t = proj.shape[0]
    ts = ROW_TILE
    tps = seq // ts
    has_vres = v_first is not None
    col = lambda c: pl.BlockSpec((ts, WIDTH), lambda i, c=c: (i, c))
    halo = lambda c: pl.BlockSpec((8, WIDTH), lambda i, c=c: (jnp.maximum(i * (ts // 8) - 1, 0), c))
    full = lambda shape: pl.BlockSpec(shape, lambda i: tuple(0 for _ in shape))
    p1c = 5120 // WIDTH
    in_specs = [col(0), col(1), col(2), col(p1c), col(p1c + 1),
                halo(0), halo(1), halo(2), halo(p1c + 1),
                full((3, WIDTH)), full((8, WIDTH)), full((128, 2 * WIDTH)),
                full((128, WIDTH)), full((128, WIDTH)), full((WIDTH, WIDTH))]
    args = [proj] * 9 + [mu_rkv, vecs, wwa, g2, v2p, ones_bd]
    if has_vres:
        in_specs.append(pl.BlockSpec((ts, WIDTH), lambda i: (i, 0)))
        args.append(v_first)
    head_spec = pl.BlockSpec((1, HEADS, ts, HEAD_DIM), lambda i: (i // tps, 0, i % tps, 0))
    head_shape = jax.ShapeDtypeStruct((t // seq, HEADS, seq, HEAD_DIM), F32)
    return pl.pallas_call(
        functools.partial(_rwkv_prep_kernel, has_vres, tps),
        grid=(t // ts,),
        in_specs=in_specs,
        out_specs=[head_spec] * 6 + [pl.BlockSpec((ts, WIDTH), lambda i: (i, 0))] * 3,
        out_shape=[head_shape] * 6 + [jax.ShapeDtypeStruct((t, WIDTH), F32)] * 3,
        compiler_params=_cparams("parallel"),
    )(*args)


def _rwkv_chunk_prep_kernel(r_ref, lw_ref, k_ref, v_ref, kk_ref, b_ref,
                            atp_ref, rt_ref, arb_ref, bh_ref, w2_ref, y2_ref, z2_ref, gc_ref):
    c_len = CHUNK
    n_chunks = r_ref.shape[1] // c_len
    row = lax.broadcasted_iota(jnp.int32, (c_len, c_len), 0)
    col = lax.broadcasted_iota(jnp.int32, (c_len, c_len), 1)
    strict = row > col
    incl = row >= col
    tril = incl.astype(BF16)
    m16 = (row // 16) == (col // 16)
    m32 = (row // 32) == (col // 32)
    eye = (row == col).astype(F32)

    chunks = range(n_chunks)
    sls = [pl.ds(c * c_len, c_len) for c in chunks]
    each = lambda fn, *cols: [fn(*args) for args in zip(*cols)]
    lw = [lw_ref[0, sl, :] for sl in sls]
    v = [v_ref[0, sl, :] for sl in sls]
    cum = each(lambda x: _dot_rsplit(tril, x), lw)
    e_pos = each(jnp.exp, cum)
    e_neg = each(lambda x: jnp.exp(-x), cum)
    g_last = [x[c_len - 1:c_len, :] for x in e_pos]
    at = each(lambda sl, x, y: -kk_ref[0, sl, :] * jnp.exp(x - y), sls, cum, lw)
    rt = each(lambda sl, x: r_ref[0, sl, :] * x, sls, e_pos)
    bt = each(lambda sl, x: b_ref[0, sl, :] * x, sls, e_neg)
    kt = each(lambda sl, x: k_ref[0, sl, :] * x, sls, e_neg)
    ar = each(lambda x, y: jnp.concatenate([x, y], axis=0).astype(BF16), at, rt)
    mb = each(_dot_nt, ar, bt)
    mk = each(_dot_nt, ar, kt)
    aab = [jnp.where(strict, x[:c_len], 0.0) for x in mb]
    arb = [jnp.where(incl, x[c_len:], 0.0) for x in mb]
    aak = [jnp.where(strict, x[:c_len], 0.0) for x in mk]
    ark = [jnp.where(incl, x[c_len:], 0.0) for x in mk]
    d1 = [jnp.where(m16, x, 0.0) for x in aab]
    d2 = each(_dot, d1, d1)
    t1 = each(lambda x, y: _dot(eye + x, eye + y), d1, d2)
    d4 = each(_dot, d2, d2)
    t2 = each(lambda x, y: _dot(x, eye + y), t1, d4)
    d8 = each(_dot, d4, d4)
    t3 = each(lambda x, y: _dot(x, eye + y), t2, d8)
    x32 = each(lambda t, a: _dot(t, jnp.where(m32 & jnp.logical_not(m16), a, 0.0)), t3, aab)
    t32 = each(lambda t, x: t + _dot(x, t), t3, x32)
    x64 = each(lambda t, a: _dot(t, jnp.where(m32, 0.0, a)), t32, aab)
    tinv = each(lambda t, x: t + _dot(x, t), t32, x64)
    akv = each(_dot, aak, v)
    atp = each(_dot, tinv, at)
    w2 = each(_dot, tinv, akv)
    y2 = each(_dot, ark, v)
    z2 = each(lambda x, y, g: _dot_tn(x, y * g), v, kt, g_last)
    for c in chunks:
        sl = sls[c]
        atp_ref[0, sl, :] = atp[c].astype(BF16)
        rt_ref[0, sl, :] = rt[c].astype(BF16)
        arb_ref[0, sl, :] = arb[c].astype(BF16)
        bh_ref[0, sl, :] = (bt[c] * g_last[c]).astype(BF16)
        w2_ref[0, sl, :] = w2[c]
        y2_ref[0, sl, :] = y2[c]
        z2_ref[0, sl, :] = z2[c]
        gc_ref[0, c:c + 1, :] = g_last[c]


def _rwkv_scan_kernel(atp_ref, rt_ref, arb_ref, bh_ref, w2_ref, y2_ref, z2_ref, gc_ref,
                      y_ref, s_ref):
    c_len = CHUNK
    n_chunks = atp_ref.shape[1] // c_len
    group = atp_ref.shape[0]

    @pl.when(pl.program_id(1) == 0)
    def _():
        s_ref[...] = jnp.zeros_like(s_ref)

    def body(c, states):
        sl = pl.ds(pl.multiple_of(c * c_len, c_len), c_len)
        heads = range(group)
        sb = [states[gi].astype(BF16) for gi in heads]
        u = [_dot_nt(atp_ref[gi, sl, :], sb[gi]) + w2_ref[gi, sl, :] for gi in heads]
        ys = [_dot_nt(rt_ref[gi, sl, :], sb[gi]) + y2_ref[gi, sl, :] for gi in heads]
        ub = [x.astype(BF16) for x in u]
        upd = [_dot_tn(ub[gi], bh_ref[gi, sl, :]) for gi in heads]
        for gi in heads:
            y_ref[gi, sl, :] = ys[gi] + _dot(arb_ref[gi, sl, :], ub[gi])
        return tuple(states[gi] * gc_ref[gi, pl.ds(c, 1), :] + upd[gi] + z2_ref[gi, sl, :]
                     for gi in heads)

    states = lax.fori_loop(0, n_chunks, body, tuple(s_ref[gi] for gi in range(group)))
    for gi in range(group):
        s_ref[gi] = states[gi]


def _rwkv_scan(r, lw, k, v, kk, b):
    bh, s, n = r.shape
    sb = min(SCAN_BLOCK, s)
    nc = sb // CHUNK
    blk = pl.BlockSpec((1, sb, n), lambda i, j: (i, j, 0))
    gblk = pl.BlockSpec((1, nc, n), lambda i, j: (i, j, 0))
    outs = pl.pallas_call(
        _rwkv_chunk_prep_kernel,
        grid=(bh, s // sb),
        in_specs=[blk] * 6,
        out_specs=[blk] * 7 + [gblk],
        out_shape=[jax.ShapeDtypeStruct((bh, s, n), BF16)] * 4
        + [jax.ShapeDtypeStruct((bh, s, n), F32)] * 3
        + [jax.ShapeDtypeStruct((bh, s // CHUNK, n), F32)],
        compiler_params=_cparams("parallel", "parallel"),
    )(r, lw, k, v, kk, b)
    grp = SCAN_GROUP
    blk = pl.BlockSpec((grp, sb, n), lambda i, j: (i, j, 0))
    gblk = pl.BlockSpec((grp, nc, n), lambda i, j: (i, j, 0))
    return pl.pallas_call(
        _rwkv_scan_kernel,
        grid=(bh // grp, s // sb),
        in_specs=[blk] * 7 + [gblk],
        out_specs=blk,
        out_shape=jax.ShapeDtypeStruct((bh, s, n), F32),
        scratch_shapes=[pltpu.VMEM((grp, n, n), F32)],
        compiler_params=_cparams("parallel", "arbitrary"),
    )(*outs)


def _moba_prep_kernel(q_ref, k_ref, v_ref, gain_ref, ones_ref, qo_ref, qt_ref, ko_ref, vt_ref, km_ref):
    ones_bd = ones_ref[...]
    q = q_ref[...]
    k = k_ref[...]
    inv_d = 1.0 / HEAD_DIM
    qn = q * lax.rsqrt(_dot_lsplit(q * q, ones_bd) * inv_d + QK_EPS) * gain_ref[0:1, :]
    kn = k * lax.rsqrt(_dot_lsplit(k * k, ones_bd) * inv_d + QK_EPS) * gain_ref[1:2, :]
    qs = qn * (HEAD_DIM ** -0.5 * LOG2E)
    qo_ref[...] = qs
    qt_ref[0] = qs.T.astype(BF16)
    _store_heads(ko_ref, kn)
    vt_ref[0] = v_ref[...].T.astype(BF16)
    km_ref[0] = jnp.mean(kn, axis=0, keepdims=True)


def _moba_prep(proj, seq, gains, ones_bd):
    t = proj.shape[0]
    ts = MOBA_BLOCK
    tps = seq // ts
    col = lambda c: pl.BlockSpec((ts, WIDTH), lambda i, c=c: (i, c))
    tspec = pl.BlockSpec((1, WIDTH, ts), lambda i: (i // tps, 0, i % tps))
    tshape = jax.ShapeDtypeStruct((t // seq, WIDTH, seq), BF16)
    return pl.pallas_call(
        _moba_prep_kernel,
        grid=(t // ts,),
        in_specs=[col(3), col(4), col(5),
                  pl.BlockSpec((2, WIDTH), lambda i: (0, 0)),
                  pl.BlockSpec((WIDTH, WIDTH), lambda i: (0, 0))],
        out_specs=[pl.BlockSpec((ts, WIDTH), lambda i: (i, 0)), tspec,
                   pl.BlockSpec((1, HEADS, ts, HEAD_DIM), lambda i: (i // tps, 0, i % tps, 0)),
                   tspec, pl.BlockSpec((1, 1, WIDTH), lambda i: (i, 0, 0))],
        out_shape=[jax.ShapeDtypeStruct((t, WIDTH), F32), tshape,
                   jax.ShapeDtypeStruct((t // seq, HEADS, seq, HEAD_DIM), BF16),
                   tshape, jax.ShapeDtypeStruct((t // ts, 1, WIDTH), F32)],
        compiler_params=_cparams("parallel"),
    )(proj, proj, proj, gains, ones_bd)


def _gating_kernel(q_ref, km_ref, o_ref):
    ah, al = _split(km_ref[0])
    bh, bl = _split(q_ref[...])
    nt = lambda a, b: lax.dot_general(a, b, (((1,), (1,)), ((), ())), preferred_element_type=F32)
    o_ref[0] = nt(ah, bh) + nt(al, bh) + nt(ah, bl)


def _gating_scores(qf, km_t, batch):
    t = qf.shape[0]
    ts = ROW_TILE
    per = t // batch // ts
    rows = km_t.shape[1]
    return pl.pallas_call(
        _gating_kernel,
        grid=(batch, per),
        in_specs=[pl.BlockSpec((ts, WIDTH), lambda b, i: (b * per + i, 0)),
                  pl.BlockSpec((1, rows, WIDTH), lambda b, i: (b, 0, 0))],
        out_specs=pl.BlockSpec((1, rows, ts), lambda b, i: (b, 0, i)),
        out_shape=jax.ShapeDtypeStruct((batch, rows, t // batch), F32),
        compiler_params=_cparams("parallel", "parallel"),
    )(qf, km_t)


def _rel_bias_kernel(tab_ref, o_ref):
    h = pl.program_id(0)
    blk = MOBA_BLOCK
    key = lax.broadcasted_iota(jnp.int32, (blk, blk), 0)
    qry = lax.broadcasted_iota(jnp.int32, (blk, blk), 1)
    max_exact = REL_BUCKETS // 2

    def lookup(n):
        nf = jnp.maximum(n, 1).astype(F32)
        large = max_exact + (jnp.log(nf / max_exact) / math.log(REL_MAX_DIST / max_exact)
                             * (REL_BUCKETS - max_exact)).astype(jnp.int32)
        bucket = jnp.where(n < max_exact, n, jnp.minimum(large, REL_BUCKETS - 1))
        out = jnp.zeros((blk, blk), F32)
        for bkt in range(REL_BUCKETS):
            out = jnp.where(bucket == bkt, tab_ref[bkt, h], out)
        return out

    dist = qry - key
    far = lookup(dist + 2 * blk)
    o_ref[0, 0] = jnp.where(dist >= 0, (lookup(jnp.maximum(dist, 0)) - far) * LOG2E, NEG_BIG)
    o_ref[0, 1] = (lookup(dist + blk) - far) * LOG2E


def _rel_bias_tiles(rel_bias):
    return pl.pallas_call(
        _rel_bias_kernel,
        grid=(HEADS,),
        in_specs=[pl.BlockSpec(memory_space=pltpu.SMEM)],
        out_specs=pl.BlockSpec((1, 2, MOBA_BLOCK, MOBA_BLOCK), lambda h: (h, 0, 0, 0)),
        out_shape=jax.ShapeDtypeStruct((HEADS, 2, MOBA_BLOCK, MOBA_BLOCK), F32),
        compiler_params=_cparams("parallel"),
    )(rel_bias)


def _moba_attn_kernel(qt_ref, k_ref, vt_ref, sc_ref, bias_ref, o_ref, mask_ref):
    blk = MOBA_BLOCK
    i = pl.program_id(1)
    nb = sc_ref.shape[1]
    n_heads = qt_ref.shape[0]
    bidx = lax.broadcasted_iota(jnp.int32, (nb, blk), 0)
    past = bidx < i
    own = pl.ds(pl.multiple_of(i * blk, blk), blk)

    carries = []
    for hh in range(n_heads):
        scm = jnp.where(past, sc_ref[hh], -jnp.inf)
        sel = jnp.zeros((nb, blk), jnp.bool_)
        for _ in range(MOBA_TOPK):
            top = jnp.max(scm, axis=0, keepdims=True)
            idx = jnp.min(jnp.where(scm == top, bidx, nb), axis=0, keepdims=True)
            pick = bidx == idx
            sel = jnp.logical_or(sel, pick)
            scm = jnp.where(pick, -jnp.inf, scm)
        mask_ref[hh] = jnp.where(jnp.logical_and(sel, past), 0.0, NEG_BIG)

        s = jnp.dot(k_ref[hh, own, :], qt_ref[hh], preferred_element_type=F32) + bias_ref[hh, 0]
        m = jnp.max(s, axis=0, keepdims=True)
        p = jnp.exp2(s - m)
        l = jnp.sum(p, axis=0, keepdims=True)
        acc = jnp.dot(vt_ref[hh, :, own], p.astype(BF16), preferred_element_type=F32)
        carries.append((m, l, acc))

    def scores(hh, j):
        sl = pl.ds(pl.multiple_of(j * blk, blk), blk)
        return jnp.dot(k_ref[hh, sl, :], qt_ref[hh], preferred_element_type=F32)

    def block_update(hh, j, s, carry):
        m, l, acc = carry
        sl = pl.ds(pl.multiple_of(j * blk, blk), blk)
        mrow = mask_ref[hh, pl.ds(j, 1), :]
        m_new = jnp.maximum(m, jnp.max(s, axis=0, keepdims=True) + mrow)
        alpha = jnp.exp2(m - m_new)
        p = jnp.exp2(s - (m_new - mrow))
        l = alpha * l + jnp.sum(p, axis=0, keepdims=True)
        acc = alpha * acc + jnp.dot(vt_ref[hh, :, sl], p.astype(BF16), preferred_element_type=F32)
        return m_new, l, acc

    def with_prev(carry):
        return tuple(block_update(hh, i - 1, scores(hh, i - 1) + bias_ref[hh, 1], carry[hh])
                     for hh in range(n_heads))

    carries = lax.cond(i > 0, with_prev, lambda c: c, tuple(carries))

    n_far = jnp.maximum(i - 1, 0)

    def far_body(j, carry):
        s_cur, stats = carry
        j_next = jnp.minimum(j + 1, jnp.maximum(n_far - 1, 0))
        s_next = tuple(scores(hh, j_next) for hh in range(n_heads))
        stats = tuple(block_update(hh, j, s_cur[hh], stats[hh]) for hh in range(n_heads))
        return s_next, stats

    s_first = tuple(scores(hh, 0) for hh in range(n_heads))
    _, carries = lax.fori_loop(0, n_far, far_body, (s_first, carries))
    for hh in range(n_heads):
        m, l, acc = carries[hh]
        o_ref[hh] = acc / l


def _moba_attn(qt, k, vt, sc_t, bias):
    bh, s, dh = k.shape
    nb = sc_t.shape[1]
    blk = MOBA_BLOCK
    hp = ATTN_HEADS_PER_STEP
    return pl.pallas_call(
        _moba_attn_kernel,
        grid=(bh // hp, s // blk),
        in_specs=[pl.BlockSpec((hp, dh, blk), lambda b, i: (b, 0, i)),
                  pl.BlockSpec((hp, s, dh), lambda b, i: (b, 0, 0)),
                  pl.BlockSpec((hp, dh, s), lambda b, i: (b, 0, 0)),
                  pl.BlockSpec((hp, nb, blk), lambda b, i: (b, 0, i)),
                  pl.BlockSpec((hp, 2, blk, blk), lambda b, i: (b % (HEADS // hp), 0, 0, 0))],
        out_specs=pl.BlockSpec((hp, dh, blk), lambda b, i: (b, 0, i)),
        out_shape=jax.ShapeDtypeStruct((bh, dh, s), F32),
        scratch_shapes=[pltpu.VMEM((hp, nb, blk), F32)],
        compiler_params=_cparams("parallel", "parallel"),
    )(qt, k, vt, sc_t, bias)


def _merge_router_kernel(x_ref, ys_ref, bonus_ref, g_ref, ym_ref, gr_ref, gm_ref,
                         lnx_ref, ones_ref, wr_ref, wm_ref, wo_ref, gf_ref, wrt_ref, brt_ref,
                         xo_ref, xn_ref, route_ref):
    ones_bd = ones_ref[...]
    inv_n = 1.0 / HEAD_DIM
    y = _load_heads(ys_ref)
    mean = _dot_lsplit(y, ones_bd) * inv_n
    d = y - mean
    var = _dot_lsplit(d * d, ones_bd) * inv_n
    yr = d * lax.rsqrt(var + LN_X_EPS) * lnx_ref[0:1, :] + lnx_ref[1:2, :]
    yr = (yr + bonus_ref[...]) * g_ref[...]
    h = (_sigmoid(gr_ref[...]) * _dot(yr, wr_ref[...])
         + _sigmoid(gm_ref[...]) * _dot(ym_ref[0].T, wm_ref[...]))
    x = x_ref[...] + _dot(h, wo_ref[...])
    xo_ref[...] = x

    ms = jnp.mean(x * x, axis=-1, keepdims=True)
    xn = x * lax.rsqrt(ms + NORM_EPS) * gf_ref[...]
    xn_ref[...] = xn

    logits = _dot_x3(xn, wrt_ref[...]) + brt_ref[...]
    lane = lax.broadcasted_iota(jnp.int32, logits.shape, 1)
    big = logits.shape[1]
    is_grp = lane < N_GROUPS
    lg = jnp.where(is_grp, logits, -jnp.inf)
    mg = jnp.max(lg, axis=1, keepdims=True)
    g_sel = jnp.min(jnp.where(lg == mg, lane, big), axis=1, keepdims=True)
    p_top = 1.0 / jnp.sum(jnp.where(is_grp, jnp.exp(logits - mg), 0.0), axis=1, keepdims=True)
    lo_lane = N_GROUPS + g_sel * EXPERTS_PER_GROUP
    in_grp = jnp.logical_and(lane >= lo_lane, lane < lo_lane + EXPERTS_PER_GROUP)
    le = jnp.where(in_grp, logits, -jnp.inf)
    m1 = jnp.max(le, axis=1, keepdims=True)
    i1 = jnp.min(jnp.where(le == m1, lane, big), axis=1, keepdims=True)
    le2 = jnp.where(lane == i1, -jnp.inf, le)
    m2 = jnp.max(le2, axis=1, keepdims=True)
    i2 = jnp.min(jnp.where(le2 == m2, lane, big), axis=1, keepdims=True)
    e2 = jnp.exp(m2 - m1)
    gate1 = p_top / (1.0 + e2)
    gate2 = p_top * e2 / (1.0 + e2)
    route = jnp.where(lane == 0, (i1 - N_GROUPS).astype(F32),
                      jnp.where(lane == 1, (i2 - N_GROUPS).astype(F32),
                                jnp.where(lane == 2, gate1, jnp.where(lane == 3, gate2, 0.0))))
    route_ref[...] = route


def _merge_router(x, ys, bonus, g, ym_t, proj, lnx, ones_bd, wr, wm, wo, gf, wrt, brt):
    t, d = x.shape
    tm = ROW_TILE
    tps = ys.shape[2] // tm
    row = lambda w: pl.BlockSpec((tm, w), lambda i: (i, 0))
    full = lambda shape: pl.BlockSpec(shape, lambda i: tuple(0 for _ in shape))
    return pl.pallas_call(
        _merge_router_kernel,
        grid=(t // tm,),
        in_specs=[row(d),
                  pl.BlockSpec((1, HEADS, tm, HEAD_DIM), lambda i: (i // tps, 0, i % tps, 0)),
                  row(WIDTH), row(WIDTH),
                  pl.BlockSpec((1, WIDTH, tm), lambda i: (i // tps, 0, i % tps)),
                  pl.BlockSpec((tm, d), lambda i: (i, 3)),
                  pl.BlockSpec((tm, d), lambda i: (i, 4)),
                  full((2, WIDTH)), full((WIDTH, WIDTH)), full((WIDTH, d)), full((WIDTH, d)),
                  full((d, d)), full((1, d)), full((d, 128)), full((1, 128))],
        out_specs=[row(d), row(d), row(128)],
        out_shape=[jax.ShapeDtypeStruct((t, d), F32), jax.ShapeDtypeStruct((t, d), F32),
                   jax.ShapeDtypeStruct((t, 128), F32)],
        compiler_params=_cparams("parallel"),
    )(x, ys, bonus, g, ym_t, proj, proj, lnx, ones_bd, wr, wm, wo, gf, wrt, brt)


def _moe_dispatch_kernel(pos_ref, x_ref, xs_in, xs_out, sem):
    del xs_in
    i = pl.program_id(0)
    tm = x_ref.shape[0]

    def row_copy(r, j):
        p = pos_ref[(i * tm + r) * 2 + j]
        return pltpu.make_async_copy(x_ref.at[pl.ds(r, 1), :], xs_out.at[pl.ds(p, 1), :], sem)

    def start(r, c):
        row_copy(r, 0).start()
        row_copy(r, 1).start()
        return c

    def wait(r, c):
        row_copy(r, 0).wait()
        row_copy(r, 1).wait()
        return c

    lax.fori_loop(0, tm, start, 0)
    lax.fori_loop(0, tm, wait, 0)


def _moe_dispatch(pos, xn, n_rows):
    t, d = xn.shape
    tm = ROW_TILE
    grid_spec = pltpu.PrefetchScalarGridSpec(
        num_scalar_prefetch=1,
        grid=(t // tm,),
        in_specs=[pl.BlockSpec((tm, d), lambda i, p: (i, 0)),
                  pl.BlockSpec(memory_space=pl.ANY)],
        out_specs=pl.BlockSpec(memory_space=pl.ANY),
        scratch_shapes=[pltpu.SemaphoreType.DMA(())],
    )
    return pl.pallas_call(
        _moe_dispatch_kernel,
        grid_spec=grid_spec,
        out_shape=jax.ShapeDtypeStruct((n_rows, d), F32),
        input_output_aliases={2: 0},
        compiler_params=_cparams("arbitrary"),
    )(pos, xn, jnp.zeros((n_rows, d), F32))


def _moe_ffn_kernel(texp_ref, nused_ref, x_ref, wg_ref, wu_ref, wd_ref, o_ref):
    i = pl.program_id(0)

    @pl.when(i < nused_ref[0])
    def _():
        xb = x_ref[...].astype(BF16)
        hg = jnp.dot(xb, wg_ref[0], preferred_element_type=F32)
        hu = jnp.dot(xb, wu_ref[0], preferred_element_type=F32)
        h = hg * _sigmoid(hg) * hu
        o_ref[...] = jnp.dot(h.astype(BF16), wd_ref[0], preferred_element_type=F32)

    @pl.when(i >= nused_ref[0])
    def _():
        o_ref[...] = jnp.zeros_like(o_ref)


def _moe_ffn(tile_expert, nused, xs, wg, wu, wd):
    n_rows, d = xs.shape
    tm = MOE_TILE
    dh = wg.shape[2]
    grid_spec = pltpu.PrefetchScalarGridSpec(
        num_scalar_prefetch=2,
        grid=(n_rows // tm,),
        in_specs=[pl.BlockSpec((tm, d), lambda i, te, nu: (jnp.minimum(i, nu[0] - 1), 0)),
                  pl.BlockSpec((1, d, dh), lambda i, te, nu: (te[i], 0, 0)),
                  pl.BlockSpec((1, d, dh), lambda i, te, nu: (te[i], 0, 0)),
                  pl.BlockSpec((1, dh, d), lambda i, te, nu: (te[i], 0, 0))],
        out_specs=pl.BlockSpec((tm, d), lambda i, te, nu: (i, 0)),
    )
    return pl.pallas_call(
        _moe_ffn_kernel,
        grid_spec=grid_spec,
        out_shape=jax.ShapeDtypeStruct((n_rows, d), F32),
        compiler_params=_cparams("arbitrary"),
    )(tile_expert, nused, xs, wg, wu, wd)


def _moe_combine_kernel(pos_ref, ys_hbm, x_ref, route_ref, o_ref, buf0, buf1, sem):
    i = pl.program_id(0)
    tm = buf0.shape[0]

    def row_copy(r, j, buf):
        p = pos_ref[(i * tm + r) * 2 + j]
        return pltpu.make_async_copy(ys_hbm.at[pl.ds(p, 1), :], buf.at[pl.ds(r, 1), :], sem)

    def start(r, c):
        row_copy(r, 0, buf0).start()
        row_copy(r, 1, buf1).start()
        return c

    def wait(r, c):
        row_copy(r, 0, buf0).wait()
        row_copy(r, 1, buf1).wait()
        return c

    lax.fori_loop(0, tm, start, 0)
    lax.fori_loop(0, tm, wait, 0)
    route = route_ref[...]
    o_ref[...] = x_ref[...] + route[:, 2:3] * buf0[...] + route[:, 3:4] * buf1[...]


def _moe_combine(pos, ys, x, route):
    t, d = x.shape
    tm = ROW_TILE
    grid_spec = pltpu.PrefetchScalarGridSpec(
        num_scalar_prefetch=1,
        grid=(t // tm,),
        in_specs=[pl.BlockSpec(memory_space=pl.ANY),
                  pl.BlockSpec((tm, d), lambda i, p: (i, 0)),
                  pl.BlockSpec((tm, 128), lambda i, p: (i, 0))],
        out_specs=pl.BlockSpec((tm, d), lambda i, p: (i, 0)),
        scratch_shapes=[pltpu.VMEM((tm, d), F32), pltpu.VMEM((tm, d), F32),
                        pltpu.SemaphoreType.DMA(())],
    )
    return pl.pallas_call(
        _moe_combine_kernel,
        grid_spec=grid_spec,
        out_shape=jax.ShapeDtypeStruct((t, d), F32),
        compiler_params=_cparams("arbitrary"),
    )(pos, ys, x, route)


def _moe_plan(route):
    tm = MOE_TILE
    e_flat = route[:, 0:2].astype(jnp.int32).reshape(-1)
    n_rows = e_flat.shape[0] + N_EXPERTS * tm
    onehot = (e_flat[:, None] == jnp.arange(N_EXPERTS, dtype=jnp.int32)[None, :]).astype(jnp.int32)
    running = jnp.cumsum(onehot, axis=0)
    counts = running[-1]
    padded = ((counts + tm - 1) // tm) * tm
    pend = jnp.cumsum(padded)
    pstart = pend - padded
    pos = jnp.sum(onehot * (running - 1 + pstart[None, :]), axis=1).astype(jnp.int32)
    tile_start = jnp.arange(n_rows // tm, dtype=jnp.int32) * tm
    tile_expert = jnp.minimum(jnp.sum(pend[None, :] <= tile_start[:, None], axis=1),
                              N_EXPERTS - 1).astype(jnp.int32)
    nused = (pend[-1:] // tm).astype(jnp.int32)
    return tile_expert, nused, pos, n_rows


def _pad_cols(w, width):
    return jnp.pad(w, ((0, 0), (0, width - w.shape[1])))


def _pad_rows(w, rows):
    return jnp.pad(w, ((0, rows - w.shape[0]), (0, 0)))


def kernel(x, norm_mix_g, norm_ffn_g, w_in, mu_rkv, mu_x, w0, w1, w2, a0, a1, a2, g1, g2, k_k, k_a, r_k, lnx_g, lnx_b, vres_mu, vres_v0, vres_v1, vres_v2, q_gain, k_gain, rel_bias, w_br_rwkv, w_br_moba, w_out, w_grp, b_grp, w_exp, b_exp, w_gate, w_up, w_down):
    batch, seq, d_model = x.shape
    depth = w_in.shape[0]
    t = batch * seq
    nb = seq // MOBA_BLOCK
    assert seq % MOBA_BLOCK == 0 and seq % SCAN_BLOCK == 0 and (batch * HEADS) % SCAN_GROUP == 0

    head_of = jnp.arange(WIDTH) // HEAD_DIM
    ones_bd = (head_of[:, None] == head_of[None, :]).astype(BF16)
    bias_tiles = _rel_bias_tiles(rel_bias)
    gains = jnp.stack([jnp.tile(q_gain, (1, HEADS)), jnp.tile(k_gain, (1, HEADS))], axis=1)

    xf = x.reshape(t, d_model)
    v_first = None
    for l in range(depth):
        has_vres = l > 0
        lora_w = [w1[l], a1[l], g1[l]]
        lora_mu = [mu_x[l, 0], mu_x[l, 1], mu_x[l, 2]]
        if has_vres:
            lora_w.append(vres_v1[l - 1])
            lora_mu.append(vres_mu[l - 1])
        p1 = _pad_cols(jnp.concatenate([w * (1.0 - m)[:, None] for w, m in zip(lora_w, lora_mu)], 1),
                       LORA_PAD)
        p2 = _pad_cols(jnp.concatenate([w * m[:, None] for w, m in zip(lora_w, lora_mu)], 1), LORA_PAD)
        w_all = jnp.concatenate([w_in[l], p1, p2], axis=1).astype(BF16)
        wwa = jnp.concatenate([
            jnp.concatenate([w2[l], jnp.zeros((DECAY_LORA, WIDTH), F32)], 1),
            jnp.concatenate([jnp.zeros((AAA_LORA, WIDTH), F32), a2[l]], 1)], 0).astype(BF16)
        v2p = (_pad_rows(vres_v2[l - 1], 128) if has_vres else jnp.zeros((128, WIDTH), F32)).astype(BF16)
        v0 = vres_v0[l - 1] if has_vres else jnp.zeros((WIDTH,), F32)
        vecs = jnp.stack([w0[l], a0[l], k_k[l], k_a[l], r_k[l].reshape(-1), v0,
                          jnp.zeros((WIDTH,), F32), jnp.zeros((WIDTH,), F32)])

        proj = _norm_proj(xf, norm_mix_g[l], w_all)

        *heads, v_row, g, bonus = _rwkv_prep(proj, seq, mu_rkv[l], vecs, wwa,
                                             g2[l].astype(BF16), v2p, ones_bd, v_first)
        if l == 0:
            v_first = v_row
        bh = batch * HEADS
        y_scan = _rwkv_scan(*[a_.reshape(bh, seq, HEAD_DIM) for a_ in heads]).reshape(
            batch, HEADS, seq, HEAD_DIM)

        qf, q_t, k_h, v_t, kmean = _moba_prep(proj, seq, gains[l], ones_bd)
        km = kmean.reshape(batch, nb, HEADS, HEAD_DIM)
        km_t = jnp.einsum("bjhd,hg->bgjhd", km, jnp.eye(HEADS, dtype=F32)).reshape(
            batch, HEADS * nb, WIDTH)
        sc_t = _gating_scores(qf, km_t, batch).reshape(bh, nb, seq)
        y_m = _moba_attn(q_t.reshape(bh, HEAD_DIM, seq), k_h.reshape(bh, seq, HEAD_DIM),
                         v_t.reshape(bh, HEAD_DIM, seq), sc_t, bias_tiles).reshape(
                             batch, WIDTH, seq)

        wrt = _pad_cols(jnp.concatenate([w_grp[l], w_exp[l]], axis=1), 128)
        brt = _pad_cols(jnp.concatenate([b_grp[l], b_exp[l]])[None, :], 128)
        lnx = jnp.stack([lnx_g[l], lnx_b[l]])
        xf, xn2, route = _merge_router(
            xf, y_scan, bonus, g, y_m, proj, lnx, ones_bd,
            w_br_rwkv[l].astype(BF16), w_br_moba[l].astype(BF16), w_out[l].astype(BF16),
            norm_ffn_g[l].reshape(1, d_model), wrt, brt)

        tile_expert, nused, pos, n_rows = _moe_plan(route)
        xs = _moe_dispatch(pos, xn2, n_rows)
        ys = _moe_ffn(tile_expert, nused, xs,
                      w_gate[l].astype(BF16), w_up[l].astype(BF16), w_down[l].astype(BF16))
        xf = _moe_combine(pos, ys, xf, route)
    return xf.reshape(batch, seq, d_model)
```

```python
import functools
import math

import jax
import jax.numpy as jnp
from jax import lax
from jax.experimental import pallas as pl
from jax.experimental.pallas import tpu as pltpu

F32 = jnp.float32
BF16 = jnp.bfloat16

HEADS = 8
HEAD_DIM = 64
WIDTH = HEADS * HEAD_DIM
DECAY_LORA, AAA_LORA, MV_LORA, GATE_LORA = 64, 64, 32, 128
LN_X_EPS = 64e-5
MOBA_BLOCK = 256
MOBA_TOPK = 3
REL_BUCKETS = 32
REL_MAX_DIST = 128
N_GROUPS = 4
EXPERTS_PER_GROUP = 8
N_EXPERTS = N_GROUPS * EXPERTS_PER_GROUP
NORM_EPS = 1e-6
QK_EPS = 1e-6

CHUNK = 64
SCAN_BLOCK = 512
SCAN_GROUP = 8
LORA_PAD = 512
ROW_TILE = 256
MOE_TILE = 256
DMA_UNROLL = 8
NEG_BIG = -1e30
LOG2E = math.log2(math.e)
ATTN_HEADS_PER_STEP = 2
SUM_ROWS = 16
VMEM_LIMIT = 48 * 1024 * 1024


def _cparams(*sem):
    return pltpu.CompilerParams(dimension_semantics=sem, vmem_limit_bytes=VMEM_LIMIT)


def _dot(a, b):
    return jnp.dot(a.astype(BF16), b.astype(BF16), preferred_element_type=F32)


def _dot_nt(a, b):
    return lax.dot_general(a.astype(BF16), b.astype(BF16), (((1,), (1,)), ((), ())),
                           preferred_element_type=F32)


def _dot_tn(a, b):
    return lax.dot_general(a.astype(BF16), b.astype(BF16), (((0,), (0,)), ((), ())),
                           preferred_element_type=F32)


def _split(x):
    hi = x.astype(BF16)
    lo = (x - hi.astype(F32)).astype(BF16)
    return hi, lo


def _dot_lsplit(a, b_exact):
    hi, lo = _split(a)
    return (jnp.dot(hi, b_exact, preferred_element_type=F32)
            + jnp.dot(lo, b_exact, preferred_element_type=F32))


def _dot_rsplit(a_exact, b):
    hi, lo = _split(b)
    return (jnp.dot(a_exact, hi, preferred_element_type=F32)
            + jnp.dot(a_exact, lo, preferred_element_type=F32))


def _dot_x3(a, b):
    ah, al = _split(a)
    bh, bl = _split(b)
    return (jnp.dot(ah, bh, preferred_element_type=F32)
            + jnp.dot(al, bh, preferred_element_type=F32)
            + jnp.dot(ah, bl, preferred_element_type=F32))


def _sigmoid(x):
    return 1.0 / (1.0 + jnp.exp(-x))


def _norm_proj_kernel(x_ref, g_ref, w_ref, o_ref, xn_ref):
    @pl.when(pl.program_id(1) == 0)
    def _():
        x = x_ref[...]
        ms = jnp.mean(x * x, axis=-1, keepdims=True)
        xn_ref[...] = (x * lax.rsqrt(ms + NORM_EPS) * g_ref[...]).astype(BF16)

    o_ref[...] = jnp.dot(xn_ref[...], w_ref[...], preferred_element_type=F32)


def _norm_proj(x, g, w, tm=1024, tn=1024):
    m, k = x.shape
    n = w.shape[1]
    return pl.pallas_call(
        _norm_proj_kernel,
        grid=(m // tm, n // tn),
        in_specs=[pl.BlockSpec((tm, k), lambda i, j: (i, 0)),
                  pl.BlockSpec((1, k), lambda i, j: (0, 0)),
                  pl.BlockSpec((k, tn), lambda i, j: (0, j))],
        out_specs=pl.BlockSpec((tm, tn), lambda i, j: (i, j)),
        out_shape=jax.ShapeDtypeStruct((m, n), F32),
        scratch_shapes=[pltpu.VMEM((tm, k), BF16)],
        compiler_params=_cparams("parallel", "arbitrary"),
    )(x, g.reshape(1, k), w)


def _store_heads(o_ref, val):
    for h in range(HEADS):
        o_ref[0, h] = val[:, h * HEAD_DIM:(h + 1) * HEAD_DIM].astype(o_ref.dtype)


def _load_heads(ref):
    return jnp.concatenate([ref[0, h] for h in range(HEADS)], axis=1)


def _shift_rows(x, halo_ref, first):
    prev = jnp.where(first, 0.0, halo_ref[7:8, :])
    rolled = pltpu.roll(x, 1, axis=0)
    row = lax.broadcasted_iota(jnp.int32, x.shape, 0)
    return jnp.where(row == 0, prev, rolled)


def _rwkv_prep_kernel(has_vres, tiles_per_seq, *refs):
    (rp_ref, kp_ref, vp_ref, p1_ref, p2_ref, rph_ref, kph_ref, vph_ref, p2h_ref,
     mu_ref, vec_ref, wwa_ref, g2_ref, v2_ref, ones_ref) = refs[:15]
    refs = refs[15:]
    if has_vres:
        vfirst_ref = refs[0]
        refs = refs[1:]
    r_o, lw_o, k_o, v_o, kk_o, b_o, vrow_o, g_o, bonus_o = refs

    first = (pl.program_id(0) % tiles_per_seq) == 0
    rp, kp, vp = rp_ref[...], kp_ref[...], vp_ref[...]
    r = rp + (_shift_rows(rp, rph_ref, first) - rp) * mu_ref[0:1, :]
    k = kp + (_shift_rows(kp, kph_ref, first) - kp) * mu_ref[1:2, :]
    v = vp + (_shift_rows(vp, vph_ref, first) - vp) * mu_ref[2:3, :]
    lo = p1_ref[...] + _shift_rows(p2_ref[...], p2h_ref, first)

    w0, a0, k_k, k_a = vec_ref[0:1, :], vec_ref[1:2, :], vec_ref[2:3, :], vec_ref[3:4, :]
    r_k, v0 = vec_ref[4:5, :], vec_ref[5:6, :]

    h_wa = lo[:, 0:128]
    lane = lax.broadcasted_iota(jnp.int32, h_wa.shape, 1)
    h_wa = jnp.where(lane < DECAY_LORA, jnp.tanh(h_wa), h_wa)
    wa = _dot(h_wa, wwa_ref[...])
    z = -(w0 + wa[:, :WIDTH])
    softplus = jnp.maximum(z, 0.0) + jnp.log(1.0 + jnp.exp(-jnp.abs(z)))
    logw = -jnp.exp(-softplus - 0.5)
    a = _sigmoid(a0 + wa[:, WIDTH:])
    g = _dot(_sigmoid(lo[:, 128:256]), g2_ref[...])
    if has_vres:
        gate_v = _sigmoid(v0 + _dot(lo[:, 256:384], v2_ref[...]))
        v = v + (vfirst_ref[...] - v) * gate_v

    ones_bd = ones_ref[...]
    kk = k * k_k
    ss = _dot_lsplit(kk * kk, ones_bd)
    kk = kk / jnp.maximum(jnp.sqrt(ss), 1e-12)
    k = k * (1.0 + (a - 1.0) * k_a)
    bonus = _dot_lsplit(r * k * r_k, ones_bd) * v

    _store_heads(r_o, r)
    _store_heads(lw_o, logw)
    _store_heads(k_o, k)
    _store_heads(v_o, v)
    _store_heads(kk_o, kk)
    _store_heads(b_o, kk * a)
    vrow_o[...] = v
    g_o[...] = g
    bonus_o[...] = bonus


def _rwkv_prep(proj, seq, mu_rkv, vecs, wwa, g2, v2p, ones_bd, v_first):
    t = proj.shape[0]
    ts = ROW_TILE
    tps = seq // ts
    has_vres = v_first is not None
    col = lambda c: pl.BlockSpec((ts, WIDTH), lambda i, c=c: (i, c))
    halo = lambda c: pl.BlockSpec((8, WIDTH), lambda i, c=c: (jnp.maximum(i * (ts // 8) - 1, 0), c))
    full = lambda shape: pl.BlockSpec(shape, lambda i: tuple(0 for _ in shape))
    p1c = 5120 // WIDTH
    in_specs = [col(0), col(1), col(2), col(p1c), col(p1c + 1),
                halo(0), halo(1), halo(2), halo(p1c + 1),
                full((3, WIDTH)), full((8, WIDTH)), full((128, 2 * WIDTH)),
                full((128, WIDTH)), full((128, WIDTH)), full((WIDTH, WIDTH))]
    args = [proj] * 9 + [mu_rkv, vecs, wwa, g2, v2p, ones_bd]
    if has_vres:
        in_specs.append(pl.BlockSpec((ts, WIDTH), lambda i: (i, 0)))
        args.append(v_first)
    head_spec = pl.BlockSpec((1, HEADS, ts, HEAD_DIM), lambda i: (i // tps, 0, i % tps, 0))
    head_shape = jax.ShapeDtypeStruct((t // seq, HEADS, seq, HEAD_DIM), F32)
    return pl.pallas_call(
        functools.partial(_rwkv_prep_kernel, has_vres, tps),
        grid=(t // ts,),
        in_specs=in_specs,
        out_specs=[head_spec] * 6 + [pl.BlockSpec((ts, WIDTH), lambda i: (i, 0))] * 3,
        out_shape=[head_shape] * 6 + [jax.ShapeDtypeStruct((t, WIDTH), F32)] * 3,
        compiler_params=_cparams("parallel"),
    )(*args)


def _rwkv_chunk_prep_kernel(r_ref, lw_ref, k_ref, v_ref, kk_ref, b_ref,
                            atp_ref, rt_ref, arb_ref, bh_ref, w2_ref, y2_ref, z2_ref, gc_ref):
    c_len = CHUNK
    n_chunks = r_ref.shape[1] // c_len
    row = lax.broadcasted_iota(jnp.int32, (c_len, c_len), 0)
    col = lax.broadcasted_iota(jnp.int32, (c_len, c_len), 1)
    strict = row > col
    incl = row >= col
    tril = incl.astype(BF16)
    m16 = (row // 16) == (col // 16)
    m32 = (row // 32) == (col // 32)
    eye = (row == col).astype(F32)

    chunks = range(n_chunks)
    sls = [pl.ds(c * c_len, c_len) for c in chunks]
    each = lambda fn, *cols: [fn(*args) for args in zip(*cols)]
    lw = [lw_ref[0, sl, :] for sl in sls]
    v = [v_ref[0, sl, :] for sl in sls]
    cum = each(lambda x: _dot_rsplit(tril, x), lw)
    e_pos = each(jnp.exp, cum)
    e_neg = each(lambda x: jnp.exp(-x), cum)
    g_last = [x[c_len - 1:c_len, :] for x in e_pos]
    at = each(lambda sl, x, y: -kk_ref[0, sl, :] * jnp.exp(x - y), sls, cum, lw)
    rt = each(lambda sl, x: r_ref[0, sl, :] * x, sls, e_pos)
    bt = each(lambda sl, x: b_ref[0, sl, :] * x, sls, e_neg)
    kt = each(lambda sl, x: k_ref[0, sl, :] * x, sls, e_neg)
    ar = each(lambda x, y: jnp.concatenate([x, y], axis=0).astype(BF16), at, rt)
    mb = each(_dot_nt, ar, bt)
    mk = each(_dot_nt, ar, kt)
    aab = [jnp.where(strict, x[:c_len], 0.0) for x in mb]
    arb = [jnp.where(incl, x[c_len:], 0.0) for x in mb]
    aak = [jnp.where(strict, x[:c_len], 0.0) for x in mk]
    ark = [jnp.where(incl, x[c_len:], 0.0) for x in mk]
    d1 = [jnp.where(m16, x, 0.0) for x in aab]
    d2 = each(_dot, d1, d1)
    t1 = each(lambda x, y: _dot(eye + x, eye + y), d1, d2)
    d4 = each(_dot, d2, d2)
    t2 = each(lambda x, y: _dot(x, eye + y), t1, d4)
    d8 = each(_dot, d4, d4)
    t3 = each(lambda x, y: _dot(x, eye + y), t2, d8)
    x32 = each(lambda t, a: _dot(t, jnp.where(m32 & jnp.logical_not(m16), a, 0.0)), t3, aab)
    t32 = each(lambda t, x: t + _dot(x, t), t3, x32)
    x64 = each(lambda t, a: _dot(t, jnp.where(m32, 0.0, a)), t32, aab)
    tinv = each(lambda t, x: t + _dot(x, t), t32, x64)
    akv = each(_dot, aak, v)
    atp = each(_dot, tinv, at)
    w2 = each(_dot, tinv, akv)
    y2 = each(_dot, ark, v)
    z2 = each(lambda x, y, g: _dot_tn(x, y * g), v, kt, g_last)
    for c in chunks:
        sl = sls[c]
        atp_ref[0, sl, :] = atp[c].astype(BF16)
        rt_ref[0, sl, :] = rt[c].astype(BF16)
        arb_ref[0, sl, :] = arb[c].astype(BF16)
        bh_ref[0, sl, :] = (bt[c] * g_last[c]).astype(BF16)
        w2_ref[0, sl, :] = w2[c]
        y2_ref[0, sl, :] = y2[c]
        z2_ref[0, sl, :] = z2[c]
        gc_ref[0, c:c + 1, :] = g_last[c]


def _rwkv_scan_kernel(atp_ref, rt_ref, arb_ref, bh_ref, w2_ref, y2_ref, z2_ref, gc_ref,
                      y_ref, s_ref):
    c_len = CHUNK
    n_chunks = atp_ref.shape[1] // c_len
    group = atp_ref.shape[0]

    @pl.when(pl.program_id(1) == 0)
    def _():
        s_ref[...] = jnp.zeros_like(s_ref)

    def body(c, states):
        sl = pl.ds(pl.multiple_of(c * c_len, c_len), c_len)
        heads = range(group)
        sb = [states[gi].astype(BF16) for gi in heads]
        u = [_dot_nt(atp_ref[gi, sl, :], sb[gi]) + w2_ref[gi, sl, :] for gi in heads]
        ys = [_dot_nt(rt_ref[gi, sl, :], sb[gi]) + y2_ref[gi, sl, :] for gi in heads]
        ub = [x.astype(BF16) for x in u]
        upd = [_dot_tn(ub[gi], bh_ref[gi, sl, :]) for gi in heads]
        for gi in heads:
            y_ref[gi, sl, :] = ys[gi] + _dot(arb_ref[gi, sl, :], ub[gi])
        return tuple(states[gi] * gc_ref[gi, pl.ds(c, 1), :] + upd[gi] + z2_ref[gi, sl, :]
                     for gi in heads)

    states = lax.fori_loop(0, n_chunks, body, tuple(s_ref[gi] for gi in range(group)))
    for gi in range(group):
        s_ref[gi] = states[gi]


def _rwkv_scan(r, lw, k, v, kk, b):
    bh, s, n = r.shape
    sb = min(SCAN_BLOCK, s)
    nc = sb // CHUNK
    blk = pl.BlockSpec((1, sb, n), lambda i, j: (i, j, 0))
    gblk = pl.BlockSpec((1, nc, n), lambda i, j: (i, j, 0))
    outs = pl.pallas_call(
        _rwkv_chunk_prep_kernel,
        grid=(bh, s // sb),
        in_specs=[blk] * 6,
        out_specs=[blk] * 7 + [gblk],
        out_shape=[jax.ShapeDtypeStruct((bh, s, n), BF16)] * 4
        + [jax.ShapeDtypeStruct((bh, s, n), F32)] * 3
        + [jax.ShapeDtypeStruct((bh, s // CHUNK, n), F32)],
        compiler_params=_cparams("parallel", "parallel"),
    )(r, lw, k, v, kk, b)
    grp = SCAN_GROUP
    blk = pl.BlockSpec((grp, sb, n), lambda i, j: (i, j, 0))
    gblk = pl.BlockSpec((grp, nc, n), lambda i, j: (i, j, 0))
    return pl.pallas_call(
        _rwkv_scan_kernel,
        grid=(bh // grp, s // sb),
        in_specs=[blk] * 7 + [gblk],
        out_specs=blk,
        out_shape=jax.ShapeDtypeStruct((bh, s, n), F32),
        scratch_shapes=[pltpu.VMEM((grp, n, n), F32)],
        compiler_params=_cparams("parallel", "arbitrary"),
    )(*outs)


def _moba_prep_kernel(q_ref, k_ref, v_ref, gain_ref, ones_ref, qo_ref, qt_ref, ko_ref, vt_ref, km_ref):
    ones_bd = ones_ref[...]
    q = q_ref[...]
    k = k_ref[...]
    inv_d = 1.0 / HEAD_DIM
    qn = q * lax.rsqrt(_dot_lsplit(q * q, ones_bd) * inv_d + QK_EPS) * gain_ref[0:1, :]
    kn = k * lax.rsqrt(_dot_lsplit(k * k, ones_bd) * inv_d + QK_EPS) * gain_ref[1:2, :]
    qs = qn * (HEAD_DIM ** -0.5 * LOG2E)
    qo_ref[...] = qs
    qt_ref[0] = qs.T.astype(BF16)
    _store_heads(ko_ref, kn)
    vt_ref[0] = v_ref[...].T.astype(BF16)
    km_ref[0] = jnp.mean(kn, axis=0, keepdims=True)


def _moba_prep(proj, seq, gains, ones_bd):
    t = proj.shape[0]
    ts = MOBA_BLOCK
    tps = seq // ts
    col = lambda c: pl.BlockSpec((ts, WIDTH), lambda i, c=c: (i, c))
    tspec = pl.BlockSpec((1, WIDTH, ts), lambda i: (i // tps, 0, i % tps))
    tshape = jax.ShapeDtypeStruct((t // seq, WIDTH, seq), BF16)
    return pl.pallas_call(
        _moba_prep_kernel,
        grid=(t // ts,),
        in_specs=[col(3), col(4), col(5),
                  pl.BlockSpec((2, WIDTH), lambda i: (0, 0)),
                  pl.BlockSpec((WIDTH, WIDTH), lambda i: (0, 0))],
        out_specs=[pl.BlockSpec((ts, WIDTH), lambda i: (i, 0)), tspec,
                   pl.BlockSpec((1, HEADS, ts, HEAD_DIM), lambda i: (i // tps, 0, i % tps, 0)),
                   tspec, pl.BlockSpec((1, 1, WIDTH), lambda i: (i, 0, 0))],
        out_shape=[jax.ShapeDtypeStruct((t, WIDTH), F32), tshape,
                   jax.ShapeDtypeStruct((t // seq, HEADS, seq, HEAD_DIM), BF16),
                   tshape, jax.ShapeDtypeStruct((t // ts, 1, WIDTH), F32)],
        compiler_params=_cparams("parallel"),
    )(proj, proj, proj, gains, ones_bd)


def _gating_kernel(q_ref, km_ref, o_ref):
    ah, al = _split(km_ref[0])
    bh, bl = _split(q_ref[...])
    nt = lambda a, b: lax.dot_general(a, b, (((1,), (1,)), ((), ())), preferred_element_type=F32)
    o_ref[0] = nt(ah, bh) + nt(al, bh) + nt(ah, bl)


def _gating_scores(qf, km_t, batch):
    t = qf.shape[0]
    ts = ROW_TILE
    per = t // batch // ts
    rows = km_t.shape[1]
    return pl.pallas_call(
        _gating_kernel,
        grid=(batch, per),
        in_specs=[pl.BlockSpec((ts, WIDTH), lambda b, i: (b * per + i, 0)),
                  pl.BlockSpec((1, rows, WIDTH), lambda b, i: (b, 0, 0))],
        out_specs=pl.BlockSpec((1, rows, ts), lambda b, i: (b, 0, i)),
        out_shape=jax.ShapeDtypeStruct((batch, rows, t // batch), F32),
        compiler_params=_cparams("parallel", "parallel"),
    )(qf, km_t)


def _rel_bias_kernel(tab_ref, o_ref):
    h = pl.program_id(0)
    blk = MOBA_BLOCK
    key = lax.broadcasted_iota(jnp.int32, (blk, blk), 0)
    qry = lax.broadcasted_iota(jnp.int32, (blk, blk), 1)
    max_exact = REL_BUCKETS // 2

    def lookup(n):
        nf = jnp.maximum(n, 1).astype(F32)
        large = max_exact + (jnp.log(nf / max_exact) / math.log(REL_MAX_DIST / max_exact)
                             * (REL_BUCKETS - max_exact)).astype(jnp.int32)
        bucket = jnp.where(n < max_exact, n, jnp.minimum(large, REL_BUCKETS - 1))
        out = jnp.zeros((blk, blk), F32)
        for bkt in range(REL_BUCKETS):
            out = jnp.where(bucket == bkt, tab_ref[bkt, h], out)
        return out

    dist = qry - key
    far = lookup(dist + 2 * blk)
    o_ref[0, 0] = jnp.where(dist >= 0, (lookup(jnp.maximum(dist, 0)) - far) * LOG2E, NEG_BIG)
    o_ref[0, 1] = (lookup(dist + blk) - far) * LOG2E


def _rel_bias_tiles(rel_bias):
    return pl.pallas_call(
        _rel_bias_kernel,
        grid=(HEADS,),
        in_specs=[pl.BlockSpec(memory_space=pltpu.SMEM)],
        out_specs=pl.BlockSpec((1, 2, MOBA_BLOCK, MOBA_BLOCK), lambda h: (h, 0, 0, 0)),
        out_shape=jax.ShapeDtypeStruct((HEADS, 2, MOBA_BLOCK, MOBA_BLOCK), F32),
        compiler_params=_cparams("parallel"),
    )(rel_bias)


def _moba_attn_kernel(qt_ref, k_ref, vt_ref, sc_ref, bias_ref, o_ref, mask_ref, sa_ref, sb_ref):
    blk = MOBA_BLOCK
    i = pl.program_id(1)
    nb = sc_ref.shape[1]
    heads = range(qt_ref.shape[0])
    dh = vt_ref.shape[1]
    bidx = lax.broadcasted_iota(jnp.int32, (nb, blk), 0)
    past = bidx < i
    ones_rows = jnp.ones((SUM_ROWS, blk), BF16)

    def block(j):
        return pl.ds(pl.multiple_of(j * blk, blk), blk)

    def scores(hh, j):
        return jnp.dot(k_ref[hh, block(j), :], qt_ref[hh], preferred_element_type=F32)

    def weighted_values(hh, j, p):
        vals = jnp.concatenate([vt_ref[hh, :, block(j)], ones_rows], axis=0)
        return jnp.dot(vals, p, preferred_element_type=F32)

    def block_update(hh, j, s, carry, mrow):
        m, acc = carry
        m_new = jnp.maximum(m, jnp.max(s, axis=0, keepdims=True) + mrow)
        alpha = jnp.exp2(m - m_new)
        p = jnp.exp2((s - (m_new - mrow)).astype(BF16))
        return m_new, alpha * acc + weighted_values(hh, j, p)

    j_prev = jnp.maximum(i - 1, 0)
    s_own = [scores(hh, i) + bias_ref[hh, 0] for hh in heads]
    s_prev = [scores(hh, j_prev) + bias_ref[hh, 1] for hh in heads]
    n_far = jnp.maximum(i - 1, 0)
    last_far = jnp.maximum(n_far - 1, 0)

    def store_scores(s_ref, j):
        for hh in heads:
            s_ref[hh] = scores(hh, jnp.minimum(j, last_far))

    store_scores(sa_ref, 0)

    for hh in heads:
        scm = jnp.where(past, sc_ref[hh], -jnp.inf)
        sel = jnp.zeros((nb, blk), jnp.bool_)
        for _ in range(MOBA_TOPK):
            top = jnp.max(scm, axis=0, keepdims=True)
            idx = jnp.min(jnp.where(scm == top, bidx, nb), axis=0, keepdims=True)
            pick = bidx == idx
            sel = jnp.logical_or(sel, pick)
            scm = jnp.where(pick, -jnp.inf, scm)
        mask_ref[hh] = jnp.where(jnp.logical_and(sel, past), 0.0, NEG_BIG)

    carries = []
    for hh in heads:
        m = jnp.max(s_own[hh], axis=0, keepdims=True)
        p = jnp.exp2((s_own[hh] - m).astype(BF16))
        carries.append((m, weighted_values(hh, i, p)))
    for hh in heads:
        mrow = jnp.where(i > 0, mask_ref[hh, pl.ds(j_prev, 1), :], NEG_BIG)
        carries[hh] = block_update(hh, j_prev, s_prev[hh], carries[hh], mrow)

    def far_body(t, stats):
        j0 = 2 * t
        j1 = jnp.minimum(j0 + 1, last_far)
        store_scores(sb_ref, j0 + 1)
        stats = tuple(block_update(hh, j0, sa_ref[hh], stats[hh], mask_ref[hh, pl.ds(j0, 1), :])
                      for hh in heads)
        store_scores(sa_ref, j0 + 2)
        return tuple(
            block_update(hh, j1, sb_ref[hh], stats[hh],
                         jnp.where(j0 + 1 < n_far, mask_ref[hh, pl.ds(j1, 1), :], NEG_BIG))
            for hh in heads)

    carries = lax.fori_loop(0, (n_far + 1) // 2, far_body, tuple(carries))
    for hh in heads:
        acc = carries[hh][1]
        o_ref[hh] = acc[:dh] / acc[dh:dh + 1]


def _moba_attn(qt, k, vt, sc_t, bias):
    bh, s, dh = k.shape
    nb = sc_t.shape[1]
    blk = MOBA_BLOCK
    hp = ATTN_HEADS_PER_STEP
    return pl.pallas_call(
        _moba_attn_kernel,
        grid=(bh // hp, s // blk),
        in_specs=[pl.BlockSpec((hp, dh, blk), lambda b, i: (b, 0, i)),
                  pl.BlockSpec((hp, s, dh), lambda b, i: (b, 0, 0)),
                  pl.BlockSpec((hp, dh, s), lambda b, i: (b, 0, 0)),
                  pl.BlockSpec((hp, nb, blk), lambda b, i: (b, 0, i)),
                  pl.BlockSpec((hp, 2, blk, blk), lambda b, i: (b % (HEADS // hp), 0, 0, 0))],
        out_specs=pl.BlockSpec((hp, dh, blk), lambda b, i: (b, 0, i)),
        out_shape=jax.ShapeDtypeStruct((bh, dh, s), F32),
        scratch_shapes=[pltpu.VMEM((hp, nb, blk), F32), pltpu.VMEM((hp, blk, blk), F32),
                        pltpu.VMEM((hp, blk, blk), F32)],
        compiler_params=_cparams("parallel", "parallel"),
    )(qt, k, vt, sc_t, bias)


def _merge_router_kernel(x_ref, ys_ref, bonus_ref, g_ref, ym_ref, gr_ref, gm_ref,
                         lnx_ref, ones_ref, wr_ref, wm_ref, wo_ref, gf_ref, wrt_ref, brt_ref,
                         xo_ref, xn_ref, route_ref):
    ones_bd = ones_ref[...]
    inv_n = 1.0 / HEAD_DIM
    y = _load_heads(ys_ref)
    mean = _dot_lsplit(y, ones_bd) * inv_n
    d = y - mean
    var = _dot_lsplit(d * d, ones_bd) * inv_n
    yr = d * lax.rsqrt(var + LN_X_EPS) * lnx_ref[0:1, :] + lnx_ref[1:2, :]
    yr = (yr + bonus_ref[...]) * g_ref[...]
    h = (_sigmoid(gr_ref[...]) * _dot(yr, wr_ref[...])
         + _sigmoid(gm_ref[...]) * _dot(ym_ref[0].T, wm_ref[...]))
    x = x_ref[...] + _dot(h, wo_ref[...])
    xo_ref[...] = x

    ms = jnp.mean(x * x, axis=-1, keepdims=True)
    xn = x * lax.rsqrt(ms + NORM_EPS) * gf_ref[...]
    xn_ref[...] = xn

    logits = _dot_x3(xn, wrt_ref[...]) + brt_ref[...]
    lane = lax.broadcasted_iota(jnp.int32, logits.shape, 1)
    big = logits.shape[1]
    is_grp = lane < N_GROUPS
    lg = jnp.where(is_grp, logits, -jnp.inf)
    mg = jnp.max(lg, axis=1, keepdims=True)
    g_sel = jnp.min(jnp.where(lg == mg, lane, big), axis=1, keepdims=True)
    p_top = 1.0 / jnp.sum(jnp.where(is_grp, jnp.exp(logits - mg), 0.0), axis=1, keepdims=True)
    lo_lane = N_GROUPS + g_sel * EXPERTS_PER_GROUP
    in_grp = jnp.logical_and(lane >= lo_lane, lane < lo_lane + EXPERTS_PER_GROUP)
    le = jnp.where(in_grp, logits, -jnp.inf)
    m1 = jnp.max(le, axis=1, keepdims=True)
    i1 = jnp.min(jnp.where(le == m1, lane, big), axis=1, keepdims=True)
    le2 = jnp.where(lane == i1, -jnp.inf, le)
    m2 = jnp.max(le2, axis=1, keepdims=True)
    i2 = jnp.min(jnp.where(le2 == m2, lane, big), axis=1, keepdims=True)
    e2 = jnp.exp(m2 - m1)
    gate1 = p_top / (1.0 + e2)
    gate2 = p_top * e2 / (1.0 + e2)
    route = jnp.where(lane == 0, (i1 - N_GROUPS).astype(F32),
                      jnp.where(lane == 1, (i2 - N_GROUPS).astype(F32),
                                jnp.where(lane == 2, gate1, jnp.where(lane == 3, gate2, 0.0))))
    route_ref[...] = route


def _merge_router(x, ys, bonus, g, ym_t, proj, lnx, ones_bd, wr, wm, wo, gf, wrt, brt):
    t, d = x.shape
    tm = ROW_TILE
    tps = ys.shape[2] // tm
    row = lambda w: pl.BlockSpec((tm, w), lambda i: (i, 0))
    full = lambda shape: pl.BlockSpec(shape, lambda i: tuple(0 for _ in shape))
    return pl.pallas_call(
        _merge_router_kernel,
        grid=(t // tm,),
        in_specs=[row(d),
                  pl.BlockSpec((1, HEADS, tm, HEAD_DIM), lambda i: (i // tps, 0, i % tps, 0)),
                  row(WIDTH), row(WIDTH),
                  pl.BlockSpec((1, WIDTH, tm), lambda i: (i // tps, 0, i % tps)),
                  pl.BlockSpec((tm, d), lambda i: (i, 3)),
                  pl.BlockSpec((tm, d), lambda i: (i, 4)),
                  full((2, WIDTH)), full((WIDTH, WIDTH)), full((WIDTH, d)), full((WIDTH, d)),
                  full((d, d)), full((1, d)), full((d, 128)), full((1, 128))],
        out_specs=[row(d), row(d), row(128)],
        out_shape=[jax.ShapeDtypeStruct((t, d), F32), jax.ShapeDtypeStruct((t, d), F32),
                   jax.ShapeDtypeStruct((t, 128), F32)],
        compiler_params=_cparams("parallel"),
    )(x, ys, bonus, g, ym_t, proj, proj, lnx, ones_bd, wr, wm, wo, gf, wrt, brt)


def _moe_dispatch_kernel(pos_ref, x_ref, xs_in, xs_out, sem):
    del xs_in
    i = pl.program_id(0)
    tm = x_ref.shape[0]

    def row_copy(r, j):
        p = pos_ref[(i * tm + r) * 2 + j]
        return pltpu.make_async_copy(x_ref.at[pl.ds(r, 1), :], xs_out.at[pl.ds(p, 1), :], sem)

    def start(r, c):
        row_copy(r, 0).start()
        row_copy(r, 1).start()
        return c

    def wait(r, c):
        row_copy(r, 0).wait()
        row_copy(r, 1).wait()
        return c

    lax.fori_loop(0, tm, start, 0, unroll=DMA_UNROLL)
    lax.fori_loop(0, tm, wait, 0, unroll=DMA_UNROLL)


def _moe_dispatch(pos, xn, n_rows):
    t, d = xn.shape
    tm = ROW_TILE
    grid_spec = pltpu.PrefetchScalarGridSpec(
        num_scalar_prefetch=1,
        grid=(t // tm,),
        in_specs=[pl.BlockSpec((tm, d), lambda i, p: (i, 0)),
                  pl.BlockSpec(memory_space=pl.ANY)],
        out_specs=pl.BlockSpec(memory_space=pl.ANY),
        scratch_shapes=[pltpu.SemaphoreType.DMA(())],
    )
    return pl.pallas_call(
        _moe_dispatch_kernel,
        grid_spec=grid_spec,
        out_shape=jax.ShapeDtypeStruct((n_rows, d), F32),
        input_output_aliases={2: 0},
        compiler_params=_cparams("arbitrary"),
    )(pos, xn, jnp.zeros((n_rows, d), F32))


def _moe_ffn_kernel(texp_ref, nused_ref, x_ref, wg_ref, wu_ref, wd_ref, o_ref):
    i = pl.program_id(0)

    @pl.when(i < nused_ref[0])
    def _():
        xb = x_ref[...].astype(BF16)
        hg = jnp.dot(xb, wg_ref[0], preferred_element_type=F32)
        hu = jnp.dot(xb, wu_ref[0], preferred_element_type=F32)
        h = hg * _sigmoid(hg) * hu
        o_ref[...] = jnp.dot(h.astype(BF16), wd_ref[0], preferred_element_type=F32)

    @pl.when(i >= nused_ref[0])
    def _():
        o_ref[...] = jnp.zeros_like(o_ref)


def _moe_ffn(tile_expert, nused, xs, wg, wu, wd):
    n_rows, d = xs.shape
    tm = MOE_TILE
    dh = wg.shape[2]
    grid_spec = pltpu.PrefetchScalarGridSpec(
        num_scalar_prefetch=2,
        grid=(n_rows // tm,),
        in_specs=[pl.BlockSpec((tm, d), lambda i, te, nu: (jnp.minimum(i, nu[0] - 1), 0)),
                  pl.BlockSpec((1, d, dh), lambda i, te, nu: (te[i], 0, 0)),
                  pl.BlockSpec((1, d, dh), lambda i, te, nu: (te[i], 0, 0)),
                  pl.BlockSpec((1, dh, d), lambda i, te, nu: (te[i], 0, 0))],
        out_specs=pl.BlockSpec((tm, d), lambda i, te, nu: (i, 0)),
    )
    return pl.pallas_call(
        _moe_ffn_kernel,
        grid_spec=grid_spec,
        out_shape=jax.ShapeDtypeStruct((n_rows, d), F32),
        compiler_params=_cparams("arbitrary"),
    )(tile_expert, nused, xs, wg, wu, wd)


def _moe_combine_kernel(pos_ref, ys_hbm, x_ref, route_ref, o_ref, buf0, buf1, sem):
    i = pl.program_id(0)
    tm = buf0.shape[0]

    def row_copy(r, j, buf):
        p = pos_ref[(i * tm + r) * 2 + j]
        return pltpu.make_async_copy(ys_hbm.at[pl.ds(p, 1), :], buf.at[pl.ds(r, 1), :], sem)

    def start(r, c):
        row_copy(r, 0, buf0).start()
        row_copy(r, 1, buf1).start()
        return c

    def wait(r, c):
        row_copy(r, 0, buf0).wait()
        row_copy(r, 1, buf1).wait()
        return c

    lax.fori_loop(0, tm, start, 0, unroll=DMA_UNROLL)
    lax.fori_loop(0, tm, wait, 0, unroll=DMA_UNROLL)
    route = route_ref[...]
    o_ref[...] = x_ref[...] + route[:, 2:3] * buf0[...] + route[:, 3:4] * buf1[...]


def _moe_combine(pos, ys, x, route):
    t, d = x.shape
    tm = ROW_TILE
    grid_spec = pltpu.PrefetchScalarGridSpec(
        num_scalar_prefetch=1,
        grid=(t // tm,),
        in_specs=[pl.BlockSpec(memory_space=pl.ANY),
                  pl.BlockSpec((tm, d), lambda i, p: (i, 0)),
                  pl.BlockSpec((tm, 128), lambda i, p: (i, 0))],
        out_specs=pl.BlockSpec((tm, d), lambda i, p: (i, 0)),
        scratch_shapes=[pltpu.VMEM((tm, d), F32), pltpu.VMEM((tm, d), F32),
                        pltpu.SemaphoreType.DMA(())],
    )
    return pl.pallas_call(
        _moe_combine_kernel,
        grid_spec=grid_spec,
        out_shape=jax.ShapeDtypeStruct((t, d), F32),
        compiler_params=_cparams("arbitrary"),
    )(pos, ys, x, route)


def _moe_plan(route):
    tm = MOE_TILE
    e_flat = route[:, 0:2].astype(jnp.int32).reshape(-1)
    n_rows = e_flat.shape[0] + N_EXPERTS * tm
    onehot = (e_flat[:, None] == jnp.arange(N_EXPERTS, dtype=jnp.int32)[None, :]).astype(jnp.int32)
    running = jnp.cumsum(onehot, axis=0)
    counts = running[-1]
    padded = ((counts + tm - 1) // tm) * tm
    pend = jnp.cumsum(padded)
    pstart = pend - padded
    pos = jnp.sum(onehot * (running - 1 + pstart[None, :]), axis=1).astype(jnp.int32)
    tile_start = jnp.arange(n_rows // tm, dtype=jnp.int32) * tm
    tile_expert = jnp.minimum(jnp.sum(pend[None, :] <= tile_start[:, None], axis=1),
                              N_EXPERTS - 1).astype(jnp.int32)
    nused = (pend[-1:] // tm).astype(jnp.int32)
    return tile_expert, nused, pos, n_rows


def _pad_cols(w, width):
    return jnp.pad(w, ((0, 0), (0, width - w.shape[1])))


def _pad_rows(w, rows):
    return jnp.pad(w, ((0, rows - w.shape[0]), (0, 0)))


def kernel(x, norm_mix_g, norm_ffn_g, w_in, mu_rkv, mu_x, w0, w1, w2, a0, a1, a2, g1, g2, k_k, k_a, r_k, lnx_g, lnx_b, vres_mu, vres_v0, vres_v1, vres_v2, q_gain, k_gain, rel_bias, w_br_rwkv, w_br_moba, w_out, w_grp, b_grp, w_exp, b_exp, w_gate, w_up, w_down):
    batch, seq, d_model = x.shape
    depth = w_in.shape[0]
    t = batch * seq
    nb = seq // MOBA_BLOCK
    assert seq % MOBA_BLOCK == 0 and seq % SCAN_BLOCK == 0 and (batch * HEADS) % SCAN_GROUP == 0

    head_of = jnp.arange(WIDTH) // HEAD_DIM
    ones_bd = (head_of[:, None] == head_of[None, :]).astype(BF16)
    bias_tiles = _rel_bias_tiles(rel_bias)
    gains = jnp.stack([jnp.tile(q_gain, (1, HEADS)), jnp.tile(k_gain, (1, HEADS))], axis=1)

    xf = x.reshape(t, d_model)
    v_first = None
    for l in range(depth):
        has_vres = l > 0
        lora_w = [w1[l], a1[l], g1[l]]
        lora_mu = [mu_x[l, 0], mu_x[l, 1], mu_x[l, 2]]
        if has_vres:
            lora_w.append(vres_v1[l - 1])
            lora_mu.append(vres_mu[l - 1])
        p1 = _pad_cols(jnp.concatenate([w * (1.0 - m)[:, None] for w, m in zip(lora_w, lora_mu)], 1),
                       LORA_PAD)
        p2 = _pad_cols(jnp.concatenate([w * m[:, None] for w, m in zip(lora_w, lora_mu)], 1), LORA_PAD)
        w_all = jnp.concatenate([w_in[l], p1, p2], axis=1).astype(BF16)
        wwa = jnp.concatenate([
            jnp.concatenate([w2[l], jnp.zeros((DECAY_LORA, WIDTH), F32)], 1),
            jnp.concatenate([jnp.zeros((AAA_LORA, WIDTH), F32), a2[l]], 1)], 0).astype(BF16)
        v2p = (_pad_rows(vres_v2[l - 1], 128) if has_vres else jnp.zeros((128, WIDTH), F32)).astype(BF16)
        v0 = vres_v0[l - 1] if has_vres else jnp.zeros((WIDTH,), F32)
        vecs = jnp.stack([w0[l], a0[l], k_k[l], k_a[l], r_k[l].reshape(-1), v0,
                          jnp.zeros((WIDTH,), F32), jnp.zeros((WIDTH,), F32)])

        proj = _norm_proj(xf, norm_mix_g[l], w_all)

        *heads, v_row, g, bonus = _rwkv_prep(proj, seq, mu_rkv[l], vecs, wwa,
                                             g2[l].astype(BF16), v2p, ones_bd, v_first)
        if l == 0:
            v_first = v_row
        bh = batch * HEADS
        y_scan = _rwkv_scan(*[a_.reshape(bh, seq, HEAD_DIM) for a_ in heads]).reshape(
            batch, HEADS, seq, HEAD_DIM)

        qf, q_t, k_h, v_t, kmean = _moba_prep(proj, seq, gains[l], ones_bd)
        km = kmean.reshape(batch, nb, HEADS, HEAD_DIM)
        km_t = jnp.einsum("bjhd,hg->bgjhd", km, jnp.eye(HEADS, dtype=F32)).reshape(
            batch, HEADS * nb, WIDTH)
        sc_t = _gating_scores(qf, km_t, batch).reshape(bh, nb, seq)
        y_m = _moba_attn(q_t.reshape(bh, HEAD_DIM, seq), k_h.reshape(bh, seq, HEAD_DIM),
                         v_t.reshape(bh, HEAD_DIM, seq), sc_t, bias_tiles).reshape(
                             batch, WIDTH, seq)

        wrt = _pad_cols(jnp.concatenate([w_grp[l], w_exp[l]], axis=1), 128)
        brt = _pad_cols(jnp.concatenate([b_grp[l], b_exp[l]])[None, :], 128)
        lnx = jnp.stack([lnx_g[l], lnx_b[l]])
        xf, xn2, route = _merge_router(
            xf, y_scan, bonus, g, y_m, proj, lnx, ones_bd,
            w_br_rwkv[l].astype(BF16), w_br_moba[l].astype(BF16), w_out[l].astype(BF16),
            norm_ffn_g[l].reshape(1, d_model), wrt, brt)

        tile_expert, nused, pos, n_rows = _moe_plan(route)
        xs = _moe_dispatch(pos, xn2, n_rows)
        ys = _moe_ffn(tile_expert, nused, xs,
                      w_gate[l].astype(BF16), w_up[l].astype(BF16), w_down[l].astype(BF16))
        xf = _moe_combine(pos, ys, xf, route)
    return xf.reshape(batch, seq, d_model)
```

```python
import functools
import math

import jax
import jax.numpy as jnp
from jax import lax
from jax.experimental import pallas as pl
from jax.experimental.pallas import tpu as pltpu

F32 = jnp.float32
BF16 = jnp.bfloat16

HEADS = 8
HEAD_DIM = 64
WIDTH = HEADS * HEAD_DIM
DECAY_LORA, AAA_LORA, MV_LORA, GATE_LORA = 64, 64, 32, 128
LN_X_EPS = 64e-5
MOBA_BLOCK = 256
MOBA_TOPK = 3
REL_BUCKETS = 32
REL_MAX_DIST = 128
N_GROUPS = 4
EXPERTS_PER_GROUP = 8
N_EXPERTS = N_GROUPS * EXPERTS_PER_GROUP
NORM_EPS = 1e-6
QK_EPS = 1e-6

CHUNK = 64
SCAN_BLOCK = 512
SCAN_GROUP = 8
CHUNK_PREP_HEADS = 4
LORA_PAD = 512
ROW_TILE = 256
MOE_TILE = 256
DMA_UNROLL = 8
NEG_BIG = -1e30
LOG2E = math.log2(math.e)
ATTN_HEADS_PER_STEP = 2
SUM_ROWS = 16
VMEM_LIMIT = 48 * 1024 * 1024


def _cparams(*sem):
    return pltpu.CompilerParams(dimension_semantics=sem, vmem_limit_bytes=VMEM_LIMIT)


def _dot(a, b):
    return jnp.dot(a.astype(BF16), b.astype(BF16), preferred_element_type=F32)


def _dot_nt(a, b):
    return lax.dot_general(a.astype(BF16), b.astype(BF16), (((1,), (1,)), ((), ())),
                           preferred_element_type=F32)


def _dot_tn(a, b):
    return lax.dot_general(a.astype(BF16), b.astype(BF16), (((0,), (0,)), ((), ())),
                           preferred_element_type=F32)


def _split(x):
    hi = x.astype(BF16)
    lo = (x - hi.astype(F32)).astype(BF16)
    return hi, lo


def _dot_lsplit(a, b_exact):
    hi, lo = _split(a)
    return (jnp.dot(hi, b_exact, preferred_element_type=F32)
            + jnp.dot(lo, b_exact, preferred_element_type=F32))


def _dot_rsplit(a_exact, b):
    hi, lo = _split(b)
    return (jnp.dot(a_exact, hi, preferred_element_type=F32)
            + jnp.dot(a_exact, lo, preferred_element_type=F32))


def _dot_x3(a, b):
    ah, al = _split(a)
    bh, bl = _split(b)
    return (jnp.dot(ah, bh, preferred_element_type=F32)
            + jnp.dot(al, bh, preferred_element_type=F32)
            + jnp.dot(ah, bl, preferred_element_type=F32))


def _sigmoid(x):
    return 1.0 / (1.0 + jnp.exp(-x))


def _norm_proj_kernel(x_ref, g_ref, w_ref, o_ref, xn_ref):
    @pl.when(pl.program_id(1) == 0)
    def _():
        x = x_ref[...]
        ms = jnp.mean(x * x, axis=-1, keepdims=True)
        xn_ref[...] = (x * lax.rsqrt(ms + NORM_EPS) * g_ref[...]).astype(BF16)

    o_ref[...] = jnp.dot(xn_ref[...], w_ref[...], preferred_element_type=F32)


def _norm_proj(x, g, w, tm=1024, tn=1024):
    m, k = x.shape
    n = w.shape[1]
    return pl.pallas_call(
        _norm_proj_kernel,
        grid=(m // tm, n // tn),
        in_specs=[pl.BlockSpec((tm, k), lambda i, j: (i, 0)),
                  pl.BlockSpec((1, k), lambda i, j: (0, 0)),
                  pl.BlockSpec((k, tn), lambda i, j: (0, j))],
        out_specs=pl.BlockSpec((tm, tn), lambda i, j: (i, j)),
        out_shape=jax.ShapeDtypeStruct((m, n), F32),
        scratch_shapes=[pltpu.VMEM((tm, k), BF16)],
        compiler_params=_cparams("parallel", "arbitrary"),
    )(x, g.reshape(1, k), w)


def _store_heads(o_ref, val):
    for h in range(HEADS):
        o_ref[0, h] = val[:, h * HEAD_DIM:(h + 1) * HEAD_DIM].astype(o_ref.dtype)


def _load_heads(ref):
    return jnp.concatenate([ref[0, h] for h in range(HEADS)], axis=1)


def _shift_rows(x, halo_ref, first):
    prev = jnp.where(first, 0.0, halo_ref[7:8, :])
    rolled = pltpu.roll(x, 1, axis=0)
    row = lax.broadcasted_iota(jnp.int32, x.shape, 0)
    return jnp.where(row == 0, prev, rolled)


def _rwkv_prep_kernel(has_vres, tiles_per_seq, *refs):
    (rp_ref, kp_ref, vp_ref, p1_ref, p2_ref, rph_ref, kph_ref, vph_ref, p2h_ref,
     mu_ref, vec_ref, wwa_ref, g2_ref, v2_ref, ones_ref) = refs[:15]
    refs = refs[15:]
    if has_vres:
        vfirst_ref = refs[0]
        refs = refs[1:]
    r_o, lw_o, k_o, v_o, kk_o, b_o, vrow_o, g_o, bonus_o = refs

    first = (pl.program_id(0) % tiles_per_seq) == 0
    rp, kp, vp = rp_ref[...], kp_ref[...], vp_ref[...]
    r = rp + (_shift_rows(rp, rph_ref, first) - rp) * mu_ref[0:1, :]
    k = kp + (_shift_rows(kp, kph_ref, first) - kp) * mu_ref[1:2, :]
    v = vp + (_shift_rows(vp, vph_ref, first) - vp) * mu_ref[2:3, :]
    lo = p1_ref[...] + _shift_rows(p2_ref[...], p2h_ref, first)

    w0, a0, k_k, k_a = vec_ref[0:1, :], vec_ref[1:2, :], vec_ref[2:3, :], vec_ref[3:4, :]
    r_k, v0 = vec_ref[4:5, :], vec_ref[5:6, :]

    h_wa = lo[:, 0:128]
    lane = lax.broadcasted_iota(jnp.int32, h_wa.shape, 1)
    h_wa = jnp.where(lane < DECAY_LORA, jnp.tanh(h_wa), h_wa)
    wa = _dot(h_wa, wwa_ref[...])
    z = -(w0 + wa[:, :WIDTH])
    softplus = jnp.maximum(z, 0.0) + jnp.log(1.0 + jnp.exp(-jnp.abs(z)))
    logw = -jnp.exp(-softplus - 0.5)
    a = _sigmoid(a0 + wa[:, WIDTH:])
    g = _dot(_sigmoid(lo[:, 128:256]), g2_ref[...])
    if has_vres:
        gate_v = _sigmoid(v0 + _dot(lo[:, 256:384], v2_ref[...]))
        v = v + (vfirst_ref[...] - v) * gate_v

    ones_bd = ones_ref[...]
    kk = k * k_k
    ss = _dot_lsplit(kk * kk, ones_bd)
    kk = kk / jnp.maximum(jnp.sqrt(ss), 1e-12)
    k = k * (1.0 + (a - 1.0) * k_a)
    bonus = _dot_lsplit(r * k * r_k, ones_bd) * v

    _store_heads(r_o, r)
    _store_heads(lw_o, logw)
    _store_heads(k_o, k)
    _store_heads(v_o, v)
    _store_heads(kk_o, kk)
    _store_heads(b_o, kk * a)
    vrow_o[...] = v
    g_o[...] = g
    bonus_o[...] = bonus


def _rwkv_prep(proj, seq, mu_rkv, vecs, wwa, g2, v2p, ones_bd, v_first):
    t = proj.shape[0]
    ts = ROW_TILE
    tps = seq // ts
    has_vres = v_first is not None
    col = lambda c: pl.BlockSpec((ts, WIDTH), lambda i, c=c: (i, c))
    halo = lambda c: pl.BlockSpec((8, WIDTH), lambda i, c=c: (jnp.maximum(i * (ts // 8) - 1, 0), c))
    full = lambda shape: pl.BlockSpec(shape, lambda i: tuple(0 for _ in shape))
    p1c = 5120 // WIDTH
    in_specs = [col(0), col(1), col(2), col(p1c), col(p1c + 1),
                halo(0), halo(1), halo(2), halo(p1c + 1),
                full((3, WIDTH)), full((8, WIDTH)), full((128, 2 * WIDTH)),
                full((128, WIDTH)), full((128, WIDTH)), full((WIDTH, WIDTH))]
    args = [proj] * 9 + [mu_rkv, vecs, wwa, g2, v2p, ones_bd]
    if has_vres:
        in_specs.append(pl.BlockSpec((ts, WIDTH), lambda i: (i, 0)))
        args.append(v_first)
    head_spec = pl.BlockSpec((1, HEADS, ts, HEAD_DIM), lambda i: (i // tps, 0, i % tps, 0))
    head_shape = jax.ShapeDtypeStruct((t // seq, HEADS, seq, HEAD_DIM), F32)
    return pl.pallas_call(
        functools.partial(_rwkv_prep_kernel, has_vres, tps),
        grid=(t // ts,),
        in_specs=in_specs,
        out_specs=[head_spec] * 6 + [pl.BlockSpec((ts, WIDTH), lambda i: (i, 0))] * 3,
        out_shape=[head_shape] * 6 + [jax.ShapeDtypeStruct((t, WIDTH), F32)] * 3,
        compiler_params=_cparams("parallel"),
    )(*args)


def _rwkv_chunk_prep_kernel(r_ref, lw_ref, k_ref, v_ref, kk_ref, b_ref,
                            atp_ref, rt_ref, arb_ref, bh_ref, w2_ref, y2_ref, z2_ref, gc_ref):
    c_len = CHUNK
    n_chunks = r_ref.shape[1] // c_len
    row = lax.broadcasted_iota(jnp.int32, (c_len, c_len), 0)
    col = lax.broadcasted_iota(jnp.int32, (c_len, c_len), 1)
    strict = row > col
    incl = row >= col
    tril = incl.astype(BF16)
    m16 = (row // 16) == (col // 16)
    m32 = (row // 32) == (col // 32)
    eye = (row == col).astype(F32)

    chunks = range(r_ref.shape[0] * n_chunks)
    sls = [(c // n_chunks, pl.ds((c % n_chunks) * c_len, c_len)) for c in chunks]
    each = lambda fn, *cols: [fn(*args) for args in zip(*cols)]
    lw = [lw_ref[sl] for sl in sls]
    v = [v_ref[sl] for sl in sls]
    cum = each(lambda x: _dot_rsplit(tril, x), lw)
    e_pos = each(jnp.exp, cum)
    e_neg = each(lambda x: jnp.exp(-x), cum)
    g_last = [x[c_len - 1:c_len, :] for x in e_pos]
    at = each(lambda sl, x, y: -kk_ref[sl] * jnp.exp(x - y), sls, cum, lw)
    rt = each(lambda sl, x: r_ref[sl] * x, sls, e_pos)
    bt = each(lambda sl, x: b_ref[sl] * x, sls, e_neg)
    kt = each(lambda sl, x: k_ref[sl] * x, sls, e_neg)
    ar = each(lambda x, y: jnp.concatenate([x, y], axis=0).astype(BF16), at, rt)
    mb = each(_dot_nt, ar, bt)
    mk = each(_dot_nt, ar, kt)
    aab = [jnp.where(strict, x[:c_len], 0.0) for x in mb]
    arb = [jnp.where(incl, x[c_len:], 0.0) for x in mb]
    aak = [jnp.where(strict, x[:c_len], 0.0) for x in mk]
    ark = [jnp.where(incl, x[c_len:], 0.0) for x in mk]
    d1 = [jnp.where(m16, x, 0.0) for x in aab]
    d2 = each(_dot, d1, d1)
    t1 = each(lambda x, y: _dot(eye + x, eye + y), d1, d2)
    d4 = each(_dot, d2, d2)
    t2 = each(lambda x, y: _dot(x, eye + y), t1, d4)
    d8 = each(_dot, d4, d4)
    t3 = each(lambda x, y: _dot(x, eye + y), t2, d8)
    x32 = each(lambda t, a: _dot(t, jnp.where(m32 & jnp.logical_not(m16), a, 0.0)), t3, aab)
    t32 = each(lambda t, x: t + _dot(x, t), t3, x32)
    x64 = each(lambda t, a: _dot(t, jnp.where(m32, 0.0, a)), t32, aab)
    tinv = each(lambda t, x: t + _dot(x, t), t32, x64)
    akv = each(_dot, aak, v)
    atp = each(_dot, tinv, at)
    w2 = each(_dot, tinv, akv)
    y2 = each(_dot, ark, v)
    z2 = each(lambda x, y, g: _dot_tn(x, y * g), v, kt, g_last)
    for c in chunks:
        sl = sls[c]
        atp_ref[sl] = atp[c].astype(BF16)
        rt_ref[sl] = rt[c].astype(BF16)
        arb_ref[sl] = arb[c].astype(BF16)
        bh_ref[sl] = (bt[c] * g_last[c]).astype(BF16)
        w2_ref[sl] = w2[c]
        y2_ref[sl] = y2[c]
        z2_ref[sl] = z2[c]
        gc_ref[c // n_chunks, c % n_chunks:c % n_chunks + 1, :] = g_last[c]


def _rwkv_scan_kernel(atp_ref, rt_ref, arb_ref, bh_ref, w2_ref, y2_ref, z2_ref, gc_ref,
                      y_ref, s_ref):
    c_len = CHUNK
    n_chunks = atp_ref.shape[1] // c_len
    group = atp_ref.shape[0]

    @pl.when(pl.program_id(1) == 0)
    def _():
        s_ref[...] = jnp.zeros_like(s_ref)

    def body(c, states):
        sl = pl.ds(pl.multiple_of(c * c_len, c_len), c_len)
        heads = range(group)
        sb = [states[gi].astype(BF16) for gi in heads]
        u = [_dot_nt(atp_ref[gi, sl, :], sb[gi]) + w2_ref[gi, sl, :] for gi in heads]
        ys = [_dot_nt(rt_ref[gi, sl, :], sb[gi]) + y2_ref[gi, sl, :] for gi in heads]
        ub = [x.astype(BF16) for x in u]
        upd = [_dot_tn(ub[gi], bh_ref[gi, sl, :]) for gi in heads]
        for gi in heads:
            y_ref[gi, sl, :] = ys[gi] + _dot(arb_ref[gi, sl, :], ub[gi])
        return tuple(states[gi] * gc_ref[gi, pl.ds(c, 1), :] + upd[gi] + z2_ref[gi, sl, :]
                     for gi in heads)

    states = lax.fori_loop(0, n_chunks, body, tuple(s_ref[gi] for gi in range(group)))
    for gi in range(group):
        s_ref[gi] = states[gi]


def _rwkv_scan(r, lw, k, v, kk, b):
    bh, s, n = r.shape
    sb = min(SCAN_BLOCK, s)
    nc = sb // CHUNK
    hp = CHUNK_PREP_HEADS
    blk = pl.BlockSpec((hp, sb, n), lambda i, j: (i, j, 0))
    gblk = pl.BlockSpec((hp, nc, n), lambda i, j: (i, j, 0))
    outs = pl.pallas_call(
        _rwkv_chunk_prep_kernel,
        grid=(bh // hp, s // sb),
        in_specs=[blk] * 6,
        out_specs=[blk] * 7 + [gblk],
        out_shape=[jax.ShapeDtypeStruct((bh, s, n), BF16)] * 4
        + [jax.ShapeDtypeStruct((bh, s, n), F32)] * 3
        + [jax.ShapeDtypeStruct((bh, s // CHUNK, n), F32)],
        compiler_params=_cparams("parallel", "parallel"),
    )(r, lw, k, v, kk, b)
    grp = SCAN_GROUP
    blk = pl.BlockSpec((grp, sb, n), lambda i, j: (i, j, 0))
    gblk = pl.BlockSpec((grp, nc, n), lambda i, j: (i, j, 0))
    return pl.pallas_call(
        _rwkv_scan_kernel,
        grid=(bh // grp, s // sb),
        in_specs=[blk] * 7 + [gblk],
        out_specs=blk,
        out_shape=jax.ShapeDtypeStruct((bh, s, n), F32),
        scratch_shapes=[pltpu.VMEM((grp, n, n), F32)],
        compiler_params=_cparams("parallel", "arbitrary"),
    )(*outs)


def _moba_prep_kernel(q_ref, k_ref, v_ref, gain_ref, ones_ref, qo_ref, qt_ref, ko_ref, vt_ref, km_ref):
    ones_bd = ones_ref[...]
    q = q_ref[...]
    k = k_ref[...]
    inv_d = 1.0 / HEAD_DIM
    qn = q * lax.rsqrt(_dot_lsplit(q * q, ones_bd) * inv_d + QK_EPS) * gain_ref[0:1, :]
    kn = k * lax.rsqrt(_dot_lsplit(k * k, ones_bd) * inv_d + QK_EPS) * gain_ref[1:2, :]
    qs = qn * (HEAD_DIM ** -0.5 * LOG2E)
    qo_ref[...] = qs
    qt_ref[0] = qs.T.astype(BF16)
    _store_heads(ko_ref, kn)
    vt_ref[0] = v_ref[...].T.astype(BF16)
    km_ref[0] = jnp.mean(kn, axis=0, keepdims=True)


def _moba_prep(proj, seq, gains, ones_bd):
    t = proj.shape[0]
    ts = MOBA_BLOCK
    tps = seq // ts
    col = lambda c: pl.BlockSpec((ts, WIDTH), lambda i, c=c: (i, c))
    tspec = pl.BlockSpec((1, WIDTH, ts), lambda i: (i // tps, 0, i % tps))
    tshape = jax.ShapeDtypeStruct((t // seq, WIDTH, seq), BF16)
    return pl.pallas_call(
        _moba_prep_kernel,
        grid=(t // ts,),
        in_specs=[col(3), col(4), col(5),
                  pl.BlockSpec((2, WIDTH), lambda i: (0, 0)),
                  pl.BlockSpec((WIDTH, WIDTH), lambda i: (0, 0))],
        out_specs=[pl.BlockSpec((ts, WIDTH), lambda i: (i, 0)), tspec,
                   pl.BlockSpec((1, HEADS, ts, HEAD_DIM), lambda i: (i // tps, 0, i % tps, 0)),
                   tspec, pl.BlockSpec((1, 1, WIDTH), lambda i: (i, 0, 0))],
        out_shape=[jax.ShapeDtypeStruct((t, WIDTH), F32), tshape,
                   jax.ShapeDtypeStruct((t // seq, HEADS, seq, HEAD_DIM), BF16),
                   tshape, jax.ShapeDtypeStruct((t // ts, 1, WIDTH), F32)],
        compiler_params=_cparams("parallel"),
    )(proj, proj, proj, gains, ones_bd)


def _gating_kernel(q_ref, km_ref, o_ref):
    ah, al = _split(km_ref[0])
    bh, bl = _split(q_ref[...])
    nt = lambda a, b: lax.dot_general(a, b, (((1,), (1,)), ((), ())), preferred_element_type=F32)
    o_ref[0] = nt(ah, bh) + nt(al, bh) + nt(ah, bl)


def _gating_scores(qf, km_t, batch):
    t = qf.shape[0]
    ts = ROW_TILE
    per = t // batch // ts
    rows = km_t.shape[1]
    return pl.pallas_call(
        _gating_kernel,
        grid=(batch, per),
        in_specs=[pl.BlockSpec((ts, WIDTH), lambda b, i: (b * per + i, 0)),
                  pl.BlockSpec((1, rows, WIDTH), lambda b, i: (b, 0, 0))],
        out_specs=pl.BlockSpec((1, rows, ts), lambda b, i: (b, 0, i)),
        out_shape=jax.ShapeDtypeStruct((batch, rows, t // batch), F32),
        compiler_params=_cparams("parallel", "parallel"),
    )(qf, km_t)


def _rel_bias_kernel(tab_ref, o_ref):
    h = pl.program_id(0)
    blk = MOBA_BLOCK
    key = lax.broadcasted_iota(jnp.int32, (blk, blk), 0)
    qry = lax.broadcasted_iota(jnp.int32, (blk, blk), 1)
    max_exact = REL_BUCKETS // 2

    def lookup(n):
        nf = jnp.maximum(n, 1).astype(F32)
        large = max_exact + (jnp.log(nf / max_exact) / math.log(REL_MAX_DIST / max_exact)
                             * (REL_BUCKETS - max_exact)).astype(jnp.int32)
        bucket = jnp.where(n < max_exact, n, jnp.minimum(large, REL_BUCKETS - 1))
        out = jnp.zeros((blk, blk), F32)
        for bkt in range(REL_BUCKETS):
            out = jnp.where(bucket == bkt, tab_ref[bkt, h], out)
        return out

    dist = qry - key
    far = lookup(dist + 2 * blk)
    o_ref[0, 0] = jnp.where(dist >= 0, (lookup(jnp.maximum(dist, 0)) - far) * LOG2E, NEG_BIG)
    o_ref[0, 1] = (lookup(dist + blk) - far) * LOG2E


def _rel_bias_tiles(rel_bias):
    return pl.pallas_call(
        _rel_bias_kernel,
        grid=(HEADS,),
        in_specs=[pl.BlockSpec(memory_space=pltpu.SMEM)],
        out_specs=pl.BlockSpec((1, 2, MOBA_BLOCK, MOBA_BLOCK), lambda h: (h, 0, 0, 0)),
        out_shape=jax.ShapeDtypeStruct((HEADS, 2, MOBA_BLOCK, MOBA_BLOCK), F32),
        compiler_params=_cparams("parallel"),
    )(rel_bias)


def _moba_attn_kernel(qt_ref, k_ref, vt_ref, sc_ref, bias_ref, o_ref, mask_ref, sa_ref, sb_ref,
                      pend_ref):
    blk = MOBA_BLOCK
    i = pl.program_id(1)
    nb = sc_ref.shape[1]
    heads = range(qt_ref.shape[0])
    dh = vt_ref.shape[1]
    bidx = lax.broadcasted_iota(jnp.int32, (nb, blk), 0)
    past = bidx < i
    ones_rows = jnp.ones((SUM_ROWS, blk), BF16)

    def block(j):
        return pl.ds(pl.multiple_of(j * blk, blk), blk)

    def scores(hh, j):
        return jnp.dot(k_ref[hh, block(j), :], qt_ref[hh], preferred_element_type=F32)

    def weighted_values(hh, j, p):
        vals = jnp.concatenate([vt_ref[hh, :, block(j)], ones_rows], axis=0)
        return jnp.dot(vals, p, preferred_element_type=F32)

    def softmax_step(s, m, mrow):
        m_new = jnp.maximum(m, jnp.max(s, axis=0, keepdims=True) + mrow)
        return m_new, jnp.exp2(m - m_new), jnp.exp2((s - (m_new - mrow)).astype(BF16))

    j_prev = jnp.maximum(i - 1, 0)
    s_own = [scores(hh, i) + bias_ref[hh, 0] for hh in heads]
    s_prev = [scores(hh, j_prev) + bias_ref[hh, 1] for hh in heads]
    n_far = jnp.maximum(i - 1, 0)
    last_far = jnp.maximum(n_far - 1, 0)

    def store_scores(s_ref, j):
        for hh in heads:
            s_ref[hh] = scores(hh, jnp.minimum(j, last_far))

    store_scores(sa_ref, 0)

    for hh in heads:
        scm = jnp.where(past, sc_ref[hh], -jnp.inf)
        sel = jnp.zeros((nb, blk), jnp.bool_)
        for _ in range(MOBA_TOPK):
            top = jnp.max(scm, axis=0, keepdims=True)
            idx = jnp.min(jnp.where(scm == top, bidx, nb), axis=0, keepdims=True)
            pick = bidx == idx
            sel = jnp.logical_or(sel, pick)
            scm = jnp.where(pick, -jnp.inf, scm)
        mask_ref[hh] = jnp.where(jnp.logical_and(sel, past), 0.0, NEG_BIG)

    ms, accs, pend = [], [], []
    for hh in heads:
        m = jnp.max(s_own[hh], axis=0, keepdims=True)
        accs.append(weighted_values(hh, i, jnp.exp2((s_own[hh] - m).astype(BF16))))
        ms.append(m)
    for hh in heads:
        mrow = jnp.where(i > 0, mask_ref[hh, pl.ds(j_prev, 1), :], NEG_BIG)
        ms[hh], alpha, pend_ref[hh] = softmax_step(s_prev[hh], ms[hh], mrow)
        pend.append(alpha)

    def far_body(t, carry):
        j_pend, ms, accs, pend = carry
        j0 = 2 * t
        j1 = jnp.minimum(j0 + 1, last_far)
        pv_pend = [weighted_values(hh, j_pend, pend_ref[hh]) for hh in heads]
        store_scores(sb_ref, j0 + 1)
        steps = [softmax_step(sa_ref[hh], ms[hh], mask_ref[hh, pl.ds(j0, 1), :]) for hh in heads]
        accs = [pend[hh] * accs[hh] + pv_pend[hh] for hh in heads]
        pv0 = [weighted_values(hh, j0, steps[hh][2]) for hh in heads]
        store_scores(sa_ref, j0 + 2)
        new_ms, new_pend = [], []
        for hh in heads:
            mrow = jnp.where(j0 + 1 < n_far, mask_ref[hh, pl.ds(j1, 1), :], NEG_BIG)
            m, alpha, pend_ref[hh] = softmax_step(sb_ref[hh], steps[hh][0], mrow)
            new_ms.append(m)
            new_pend.append(alpha)
        accs = [steps[hh][1] * accs[hh] + pv0[hh] for hh in heads]
        return j1, tuple(new_ms), tuple(accs), tuple(new_pend)

    j_pend, ms, accs, pend = lax.fori_loop(
        0, (n_far + 1) // 2, far_body, (j_prev, tuple(ms), tuple(accs), tuple(pend)))
    for hh in heads:
        acc = pend[hh] * accs[hh] + weighted_values(hh, j_pend, pend_ref[hh])
        o_ref[hh] = acc[:dh] / acc[dh:dh + 1]


def _moba_attn(qt, k, vt, sc_t, bias):
    bh, s, dh = k.shape
    nb = sc_t.shape[1]
    blk = MOBA_BLOCK
    hp = ATTN_HEADS_PER_STEP
    return pl.pallas_call(
        _moba_attn_kernel,
        grid=(bh // hp, s // blk),
        in_specs=[pl.BlockSpec((hp, dh, blk), lambda b, i: (b, 0, i)),
                  pl.BlockSpec((hp, s, dh), lambda b, i: (b, 0, 0)),
                  pl.BlockSpec((hp, dh, s), lambda b, i: (b, 0, 0)),
                  pl.BlockSpec((hp, nb, blk), lambda b, i: (b, 0, i)),
                  pl.BlockSpec((hp, 2, blk, blk), lambda b, i: (b % (HEADS // hp), 0, 0, 0))],
        out_specs=pl.BlockSpec((hp, dh, blk), lambda b, i: (b, 0, i)),
        out_shape=jax.ShapeDtypeStruct((bh, dh, s), F32),
        scratch_shapes=[pltpu.VMEM((hp, nb, blk), F32), pltpu.VMEM((hp, blk, blk), F32),
                        pltpu.VMEM((hp, blk, blk), F32), pltpu.VMEM((hp, blk, blk), BF16)],
        compiler_params=_cparams("parallel", "parallel"),
    )(qt, k, vt, sc_t, bias)


def _merge_router_kernel(x_ref, ys_ref, bonus_ref, g_ref, ym_ref, gr_ref, gm_ref,
                         lnx_ref, ones_ref, wr_ref, wm_ref, wo_ref, gf_ref, wrt_ref, brt_ref,
                         xo_ref, xn_ref, route_ref):
    ones_bd = ones_ref[...]
    inv_n = 1.0 / HEAD_DIM
    y = _load_heads(ys_ref)
    mean = _dot_lsplit(y, ones_bd) * inv_n
    d = y - mean
    var = _dot_lsplit(d * d, ones_bd) * inv_n
    yr = d * lax.rsqrt(var + LN_X_EPS) * lnx_ref[0:1, :] + lnx_ref[1:2, :]
    yr = (yr + bonus_ref[...]) * g_ref[...]
    h = (_sigmoid(gr_ref[...]) * _dot(yr, wr_ref[...])
         + _sigmoid(gm_ref[...]) * _dot(ym_ref[0].T, wm_ref[...]))
    x = x_ref[...] + _dot(h, wo_ref[...])
    xo_ref[...] = x

    ms = jnp.mean(x * x, axis=-1, keepdims=True)
    xn = x * lax.rsqrt(ms + NORM_EPS) * gf_ref[...]
    xn_ref[...] = xn

    logits = _dot_x3(xn, wrt_ref[...]) + brt_ref[...]
    lane = lax.broadcasted_iota(jnp.int32, logits.shape, 1)
    big = logits.shape[1]
    is_grp = lane < N_GROUPS
    lg = jnp.where(is_grp, logits, -jnp.inf)
    mg = jnp.max(lg, axis=1, keepdims=True)
    g_sel = jnp.min(jnp.where(lg == mg, lane, big), axis=1, keepdims=True)
    p_top = 1.0 / jnp.sum(jnp.where(is_grp, jnp.exp(logits - mg), 0.0), axis=1, keepdims=True)
    lo_lane = N_GROUPS + g_sel * EXPERTS_PER_GROUP
    in_grp = jnp.logical_and(lane >= lo_lane, lane < lo_lane + EXPERTS_PER_GROUP)
    le = jnp.where(in_grp, logits, -jnp.inf)
    m1 = jnp.max(le, axis=1, keepdims=True)
    i1 = jnp.min(jnp.where(le == m1, lane, big), axis=1, keepdims=True)
    le2 = jnp.where(lane == i1, -jnp.inf, le)
    m2 = jnp.max(le2, axis=1, keepdims=True)
    i2 = jnp.min(jnp.where(le2 == m2, lane, big), axis=1, keepdims=True)
    e2 = jnp.exp(m2 - m1)
    gate1 = p_top / (1.0 + e2)
    gate2 = p_top * e2 / (1.0 + e2)
    route = jnp.where(lane == 0, (i1 - N_GROUPS).astype(F32),
                      jnp.where(lane == 1, (i2 - N_GROUPS).astype(F32),
                                jnp.where(lane == 2, gate1, jnp.where(lane == 3, gate2, 0.0))))
    route_ref[...] = route


def _merge_router(x, ys, bonus, g, ym_t, proj, lnx, ones_bd, wr, wm, wo, gf, wrt, brt):
    t, d = x.shape
    tm = ROW_TILE
    tps = ys.shape[2] // tm
    row = lambda w: pl.BlockSpec((tm, w), lambda i: (i, 0))
    full = lambda shape: pl.BlockSpec(shape, lambda i: tuple(0 for _ in shape))
    return pl.pallas_call(
        _merge_router_kernel,
        grid=(t // tm,),
        in_specs=[row(d),
                  pl.BlockSpec((1, HEADS, tm, HEAD_DIM), lambda i: (i // tps, 0, i % tps, 0)),
                  row(WIDTH), row(WIDTH),
                  pl.BlockSpec((1, WIDTH, tm), lambda i: (i // tps, 0, i % tps)),
                  pl.BlockSpec((tm, d), lambda i: (i, 3)),
                  pl.BlockSpec((tm, d), lambda i: (i, 4)),
                  full((2, WIDTH)), full((WIDTH, WIDTH)), full((WIDTH, d)), full((WIDTH, d)),
                  full((d, d)), full((1, d)), full((d, 128)), full((1, 128))],
        out_specs=[row(d), row(d), row(128)],
        out_shape=[jax.ShapeDtypeStruct((t, d), F32), jax.ShapeDtypeStruct((t, d), F32),
                   jax.ShapeDtypeStruct((t, 128), F32)],
        compiler_params=_cparams("parallel"),
    )(x, ys, bonus, g, ym_t, proj, proj, lnx, ones_bd, wr, wm, wo, gf, wrt, brt)


def _moe_dispatch_kernel(pos_ref, x_ref, xs_in, xs_out, sem):
    del xs_in
    i = pl.program_id(0)
    tm = x_ref.shape[0]

    def row_copy(r, j):
        p = pos_ref[(i * tm + r) * 2 + j]
        return pltpu.make_async_copy(x_ref.at[pl.ds(r, 1), :], xs_out.at[pl.ds(p, 1), :], sem)

    def start(r, c):
        row_copy(r, 0).start()
        row_copy(r, 1).start()
        return c

    def wait(r, c):
        row_copy(r, 0).wait()
        row_copy(r, 1).wait()
        return c

    lax.fori_loop(0, tm, start, 0, unroll=DMA_UNROLL)
    lax.fori_loop(0, tm, wait, 0, unroll=DMA_UNROLL)


def _moe_dispatch(pos, xn, n_rows):
    t, d = xn.shape
    tm = ROW_TILE
    grid_spec = pltpu.PrefetchScalarGridSpec(
        num_scalar_prefetch=1,
        grid=(t // tm,),
        in_specs=[pl.BlockSpec((tm, d), lambda i, p: (i, 0)),
                  pl.BlockSpec(memory_space=pl.ANY)],
        out_specs=pl.BlockSpec(memory_space=pl.ANY),
        scratch_shapes=[pltpu.SemaphoreType.DMA(())],
    )
    return pl.pallas_call(
        _moe_dispatch_kernel,
        grid_spec=grid_spec,
        out_shape=jax.ShapeDtypeStruct((n_rows, d), F32),
        input_output_aliases={2: 0},
        compiler_params=_cparams("arbitrary"),
    )(pos, xn, jnp.zeros((n_rows, d), F32))


def _moe_ffn_kernel(texp_ref, nused_ref, x_ref, wg_ref, wu_ref, wd_ref, o_ref, wg_bf, wu_bf, wd_bf):
    i = pl.program_id(0)
    used = i < nused_ref[0]
    new_expert = jnp.logical_or(i == 0, texp_ref[i] != texp_ref[jnp.maximum(i - 1, 0)])

    @pl.when(jnp.logical_and(used, new_expert))
    def _():
        wg_bf[...] = wg_ref[0].astype(BF16)
        wu_bf[...] = wu_ref[0].astype(BF16)
        wd_bf[...] = wd_ref[0].astype(BF16)

    @pl.when(used)
    def _():
        xb = x_ref[...].astype(BF16)
        hg = jnp.dot(xb, wg_bf[...], preferred_element_type=F32)
        hu = jnp.dot(xb, wu_bf[...], preferred_element_type=F32)
        h = hg * _sigmoid(hg) * hu
        o_ref[...] = jnp.dot(h.astype(BF16), wd_bf[...], preferred_element_type=F32)

    @pl.when(jnp.logical_not(used))
    def _():
        o_ref[...] = jnp.zeros_like(o_ref)


def _moe_ffn(tile_expert, nused, xs, wg, wu, wd):
    n_rows, d = xs.shape
    tm = MOE_TILE
    dh = wg.shape[2]
    grid_spec = pltpu.PrefetchScalarGridSpec(
        num_scalar_prefetch=2,
        grid=(n_rows // tm,),
        in_specs=[pl.BlockSpec((tm, d), lambda i, te, nu: (jnp.minimum(i, nu[0] - 1), 0)),
                  pl.BlockSpec((1, d, dh), lambda i, te, nu: (te[i], 0, 0)),
                  pl.BlockSpec((1, d, dh), lambda i, te, nu: (te[i], 0, 0)),
                  pl.BlockSpec((1, dh, d), lambda i, te, nu: (te[i], 0, 0))],
        out_specs=pl.BlockSpec((tm, d), lambda i, te, nu: (i, 0)),
        scratch_shapes=[pltpu.VMEM((d, dh), BF16), pltpu.VMEM((d, dh), BF16),
                        pltpu.VMEM((dh, d), BF16)],
    )
    return pl.pallas_call(
        _moe_ffn_kernel,
        grid_spec=grid_spec,
        out_shape=jax.ShapeDtypeStruct((n_rows, d), F32),
        compiler_params=_cparams("arbitrary"),
    )(tile_expert, nused, xs, wg, wu, wd)


def _moe_combine_kernel(pos_ref, ys_hbm, x_ref, route_ref, o_ref, buf0, buf1, sem):
    i = pl.program_id(0)
    tm = buf0.shape[0]

    def row_copy(r, j, buf):
        p = pos_ref[(i * tm + r) * 2 + j]
        return pltpu.make_async_copy(ys_hbm.at[pl.ds(p, 1), :], buf.at[pl.ds(r, 1), :], sem)

    def start(r, c):
        row_copy(r, 0, buf0).start()
        row_copy(r, 1, buf1).start()
        return c

    def wait(r, c):
        row_copy(r, 0, buf0).wait()
        row_copy(r, 1, buf1).wait()
        return c

    lax.fori_loop(0, tm, start, 0, unroll=DMA_UNROLL)
    lax.fori_loop(0, tm, wait, 0, unroll=DMA_UNROLL)
    route = route_ref[...]
    o_ref[...] = x_ref[...] + route[:, 2:3] * buf0[...] + route[:, 3:4] * buf1[...]


def _moe_combine(pos, ys, x, route):
    t, d = x.shape
    tm = ROW_TILE
    grid_spec = pltpu.PrefetchScalarGridSpec(
        num_scalar_prefetch=1,
        grid=(t // tm,),
        in_specs=[pl.BlockSpec(memory_space=pl.ANY),
                  pl.BlockSpec((tm, d), lambda i, p: (i, 0)),
                  pl.BlockSpec((tm, 128), lambda i, p: (i, 0))],
        out_specs=pl.BlockSpec((tm, d), lambda i, p: (i, 0)),
        scratch_shapes=[pltpu.VMEM((tm, d), F32), pltpu.VMEM((tm, d), F32),
                        pltpu.SemaphoreType.DMA(())],
    )
    return pl.pallas_call(
        _moe_combine_kernel,
        grid_spec=grid_spec,
        out_shape=jax.ShapeDtypeStruct((t, d), F32),
        compiler_params=_cparams("arbitrary"),
    )(pos, ys, x, route)


def _moe_plan(route):
    tm = MOE_TILE
    e_flat = route[:, 0:2].astype(jnp.int32).reshape(-1)
    n_rows = e_flat.shape[0] + N_EXPERTS * tm
    onehot = (e_flat[:, None] == jnp.arange(N_EXPERTS, dtype=jnp.int32)[None, :]).astype(jnp.int32)
    running = jnp.cumsum(onehot, axis=0)
    counts = running[-1]
    padded = ((counts + tm - 1) // tm) * tm
    pend = jnp.cumsum(padded)
    pstart = pend - padded
    pos = jnp.sum(onehot * (running - 1 + pstart[None, :]), axis=1).astype(jnp.int32)
    tile_start = jnp.arange(n_rows // tm, dtype=jnp.int32) * tm
    tile_expert = jnp.minimum(jnp.sum(pend[None, :] <= tile_start[:, None], axis=1),
                              N_EXPERTS - 1).astype(jnp.int32)
    nused = (pend[-1:] // tm).astype(jnp.int32)
    return tile_expert, nused, pos, n_rows


def _pad_cols(w, width):
    return jnp.pad(w, ((0, 0), (0, width - w.shape[1])))


def _pad_rows(w, rows):
    return jnp.pad(w, ((0, rows - w.shape[0]), (0, 0)))


def kernel(x, norm_mix_g, norm_ffn_g, w_in, mu_rkv, mu_x, w0, w1, w2, a0, a1, a2, g1, g2, k_k, k_a, r_k, lnx_g, lnx_b, vres_mu, vres_v0, vres_v1, vres_v2, q_gain, k_gain, rel_bias, w_br_rwkv, w_br_moba, w_out, w_grp, b_grp, w_exp, b_exp, w_gate, w_up, w_down):
    batch, seq, d_model = x.shape
    depth = w_in.shape[0]
    t = batch * seq
    nb = seq // MOBA_BLOCK
    assert seq % MOBA_BLOCK == 0 and seq % SCAN_BLOCK == 0 and (batch * HEADS) % SCAN_GROUP == 0

    head_of = jnp.arange(WIDTH) // HEAD_DIM
    ones_bd = (head_of[:, None] == head_of[None, :]).astype(BF16)
    bias_tiles = _rel_bias_tiles(rel_bias)
    gains = jnp.stack([jnp.tile(q_gain, (1, HEADS)), jnp.tile(k_gain, (1, HEADS))], axis=1)

    xf = x.reshape(t, d_model)
    v_first = None
    for l in range(depth):
        has_vres = l > 0
        lora_w = [w1[l], a1[l], g1[l]]
        lora_mu = [mu_x[l, 0], mu_x[l, 1], mu_x[l, 2]]
        if has_vres:
            lora_w.append(vres_v1[l - 1])
            lora_mu.append(vres_mu[l - 1])
        p1 = _pad_cols(jnp.concatenate([w * (1.0 - m)[:, None] for w, m in zip(lora_w, lora_mu)], 1),
                       LORA_PAD)
        p2 = _pad_cols(jnp.concatenate([w * m[:, None] for w, m in zip(lora_w, lora_mu)], 1), LORA_PAD)
        w_all = jnp.concatenate([w_in[l], p1, p2], axis=1).astype(BF16)
        wwa = jnp.concatenate([
            jnp.concatenate([w2[l], jnp.zeros((DECAY_LORA, WIDTH), F32)], 1),
            jnp.concatenate([jnp.zeros((AAA_LORA, WIDTH), F32), a2[l]], 1)], 0).astype(BF16)
        v2p = (_pad_rows(vres_v2[l - 1], 128) if has_vres else jnp.zeros((128, WIDTH), F32)).astype(BF16)
        v0 = vres_v0[l - 1] if has_vres else jnp.zeros((WIDTH,), F32)
        vecs = jnp.stack([w0[l], a0[l], k_k[l], k_a[l], r_k[l].reshape(-1), v0,
                          jnp.zeros((WIDTH,), F32), jnp.zeros((WIDTH,), F32)])

        proj = _norm_proj(xf, norm_mix_g[l], w_all)

        *heads, v_row, g, bonus = _rwkv_prep(proj, seq, mu_rkv[l], vecs, wwa,
                                             g2[l].astype(BF16), v2p, ones_bd, v_first)
        if l == 0:
            v_first = v_row
        bh = batch * HEADS
        y_scan = _rwkv_scan(*[a_.reshape(bh, seq, HEAD_DIM) for a_ in heads]).reshape(
            batch, HEADS, seq, HEAD_DIM)

        qf, q_t, k_h, v_t, kmean = _moba_prep(proj, seq, gains[l], ones_bd)
        km = kmean.reshape(batch, nb, HEADS, HEAD_DIM)
        km_t = jnp.einsum("bjhd,hg->bgjhd", km, jnp.eye(HEADS, dtype=F32)).reshape(
            batch, HEADS * nb, WIDTH)
        sc_t = _gating_scores(qf, km_t, batch).reshape(bh, nb, seq)
        y_m = _moba_attn(q_t.reshape(bh, HEAD_DIM, seq), k_h.reshape(bh, seq, HEAD_DIM),
                         v_t.reshape(bh, HEAD_DIM, seq), sc_t, bias_tiles).reshape(
                             batch, WIDTH, seq)

        wrt = _pad_cols(jnp.concatenate([w_grp[l], w_exp[l]], axis=1), 128)
        brt = _pad_cols(jnp.concatenate([b_grp[l], b_exp[l]])[None, :], 128)
        lnx = jnp.stack([lnx_g[l], lnx_b[l]])
        xf, xn2, route = _merge_router(
            xf, y_scan, bonus, g, y_m, proj, lnx, ones_bd,
            w_br_rwkv[l].astype(BF16), w_br_moba[l].astype(BF16), w_out[l].astype(BF16),
            norm_ffn_g[l].reshape(1, d_model), wrt, brt)

        tile_expert, nused, pos, n_rows = _moe_plan(route)
        xs = _moe_dispatch(pos, xn2, n_rows)
        ys = _moe_ffn(tile_expert, nused, xs, w_gate[l], w_up[l], w_down[l])
        xf = _moe_combine(pos, ys, xf, route)
    return xf.reshape(batch, seq, d_model)
```

```python
import functools
import math

import jax
import jax.numpy as jnp
from jax import lax
from jax.experimental import pallas as pl
from jax.experimental.pallas import tpu as pltpu

F32 = jnp.float32
BF16 = jnp.bfloat16

HEADS = 8
HEAD_DIM = 64
WIDTH = HEADS * HEAD_DIM
DECAY_LORA, AAA_LORA, MV_LORA, GATE_LORA = 64, 64, 32, 128
LN_X_EPS = 64e-5
MOBA_BLOCK = 256
MOBA_TOPK = 3
REL_BUCKETS = 32
REL_MAX_DIST = 128
N_GROUPS = 4
EXPERTS_PER_GROUP = 8
N_EXPERTS = N_GROUPS * EXPERTS_PER_GROUP
NORM_EPS = 1e-6
QK_EPS = 1e-6

CHUNK = 64
SCAN_BLOCK = 512
SCAN_GROUP = 8
CHUNK_PREP_HEADS = 4
LORA_PAD = 512
ROW_TILE = 256
MOE_TILE = 256
MERGE_TILE = 512
DMA_UNROLL = 8
NEG_BIG = -1e30
LOG2E = math.log2(math.e)
ATTN_HEADS_PER_STEP = 4
SUM_ROWS = 16
VMEM_LIMIT = 48 * 1024 * 1024


def _cparams(*sem):
    return pltpu.CompilerParams(dimension_semantics=sem, vmem_limit_bytes=VMEM_LIMIT)


def _dot(a, b):
    return jnp.dot(a.astype(BF16), b.astype(BF16), preferred_element_type=F32)


def _dot_nt(a, b):
    return lax.dot_general(a.astype(BF16), b.astype(BF16), (((1,), (1,)), ((), ())),
                           preferred_element_type=F32)


def _dot_tn(a, b):
    return lax.dot_general(a.astype(BF16), b.astype(BF16), (((0,), (0,)), ((), ())),
                           preferred_element_type=F32)


def _split(x):
    hi = x.astype(BF16)
    lo = (x - hi.astype(F32)).astype(BF16)
    return hi, lo


def _dot_lsplit(a, b_exact):
    hi, lo = _split(a)
    return (jnp.dot(hi, b_exact, preferred_element_type=F32)
            + jnp.dot(lo, b_exact, preferred_element_type=F32))


def _dot_rsplit(a_exact, b):
    hi, lo = _split(b)
    return (jnp.dot(a_exact, hi, preferred_element_type=F32)
            + jnp.dot(a_exact, lo, preferred_element_type=F32))


def _dot_x3(a, b):
    ah, al = _split(a)
    bh, bl = _split(b)
    return (jnp.dot(ah, bh, preferred_element_type=F32)
            + jnp.dot(al, bh, preferred_element_type=F32)
            + jnp.dot(ah, bl, preferred_element_type=F32))


def _sigmoid(x):
    return 1.0 / (1.0 + jnp.exp(-x))


def _norm_proj_kernel(x_ref, g_ref, w_ref, o_ref, xn_ref):
    @pl.when(pl.program_id(1) == 0)
    def _():
        x = x_ref[...]
        ms = jnp.mean(x * x, axis=-1, keepdims=True)
        xn_ref[...] = (x * lax.rsqrt(ms + NORM_EPS) * g_ref[...]).astype(BF16)

    o_ref[...] = jnp.dot(xn_ref[...], w_ref[...], preferred_element_type=F32)


def _norm_proj(x, g, w, tm=1024, tn=1024):
    m, k = x.shape
    n = w.shape[1]
    return pl.pallas_call(
        _norm_proj_kernel,
        grid=(m // tm, n // tn),
        in_specs=[pl.BlockSpec((tm, k), lambda i, j: (i, 0)),
                  pl.BlockSpec((1, k), lambda i, j: (0, 0)),
                  pl.BlockSpec((k, tn), lambda i, j: (0, j))],
        out_specs=pl.BlockSpec((tm, tn), lambda i, j: (i, j)),
        out_shape=jax.ShapeDtypeStruct((m, n), F32),
        scratch_shapes=[pltpu.VMEM((tm, k), BF16)],
        compiler_params=_cparams("parallel", "arbitrary"),
    )(x, g.reshape(1, k), w)


def _store_heads(o_ref, val):
    for h in range(HEADS):
        o_ref[0, h] = val[:, h * HEAD_DIM:(h + 1) * HEAD_DIM].astype(o_ref.dtype)


def _load_heads(ref):
    return jnp.concatenate([ref[0, h] for h in range(HEADS)], axis=1)


def _shift_rows(x, halo_ref, first):
    prev = jnp.where(first, 0.0, halo_ref[7:8, :])
    rolled = pltpu.roll(x, 1, axis=0)
    row = lax.broadcasted_iota(jnp.int32, x.shape, 0)
    return jnp.where(row == 0, prev, rolled)


def _rwkv_prep_kernel(has_vres, tiles_per_seq, *refs):
    (rp_ref, kp_ref, vp_ref, p1_ref, p2_ref, rph_ref, kph_ref, vph_ref, p2h_ref,
     mu_ref, vec_ref, wwa_ref, g2_ref, v2_ref, ones_ref) = refs[:15]
    refs = refs[15:]
    if has_vres:
        vfirst_ref = refs[0]
        refs = refs[1:]
    r_o, lw_o, k_o, v_o, kk_o, b_o, vrow_o, g_o, bonus_o = refs

    first = (pl.program_id(0) % tiles_per_seq) == 0
    rp, kp, vp = rp_ref[...], kp_ref[...], vp_ref[...]
    r = rp + (_shift_rows(rp, rph_ref, first) - rp) * mu_ref[0:1, :]
    k = kp + (_shift_rows(kp, kph_ref, first) - kp) * mu_ref[1:2, :]
    v = vp + (_shift_rows(vp, vph_ref, first) - vp) * mu_ref[2:3, :]
    lo = p1_ref[...] + _shift_rows(p2_ref[...], p2h_ref, first)

    w0, a0, k_k, k_a = vec_ref[0:1, :], vec_ref[1:2, :], vec_ref[2:3, :], vec_ref[3:4, :]
    r_k, v0 = vec_ref[4:5, :], vec_ref[5:6, :]

    h_wa = lo[:, 0:128]
    lane = lax.broadcasted_iota(jnp.int32, h_wa.shape, 1)
    h_wa = jnp.where(lane < DECAY_LORA, jnp.tanh(h_wa), h_wa)
    wa = _dot(h_wa, wwa_ref[...])
    z = -(w0 + wa[:, :WIDTH])
    softplus = jnp.maximum(z, 0.0) + jnp.log(1.0 + jnp.exp(-jnp.abs(z)))
    logw = -jnp.exp(-softplus - 0.5)
    a = _sigmoid(a0 + wa[:, WIDTH:])
    g = _dot(_sigmoid(lo[:, 128:256]), g2_ref[...])
    if has_vres:
        gate_v = _sigmoid(v0 + _dot(lo[:, 256:384], v2_ref[...]))
        v = v + (vfirst_ref[...] - v) * gate_v

    ones_bd = ones_ref[...]
    kk = k * k_k
    ss = _dot_lsplit(kk * kk, ones_bd)
    kk = kk / jnp.maximum(jnp.sqrt(ss), 1e-12)
    k = k * (1.0 + (a - 1.0) * k_a)
    bonus = _dot_lsplit(r * k * r_k, ones_bd) * v

    _store_heads(r_o, r)
    _store_heads(lw_o, logw)
    _store_heads(k_o, k)
    _store_heads(v_o, v)
    _store_heads(kk_o, kk)
    _store_heads(b_o, kk * a)
    vrow_o[...] = v
    g_o[...] = g
    bonus_o[...] = bonus


def _rwkv_prep(proj, seq, mu_rkv, vecs, wwa, g2, v2p, ones_bd, v_first):
    t = proj.shape[0]
    ts = ROW_TILE
    tps = seq // ts
    has_vres = v_first is not None
    col = lambda c: pl.BlockSpec((ts, WIDTH), lambda i, c=c: (i, c))
    halo = lambda c: pl.BlockSpec((8, WIDTH), lambda i, c=c: (jnp.maximum(i * (ts // 8) - 1, 0), c))
    full = lambda shape: pl.BlockSpec(shape, lambda i: tuple(0 for _ in shape))
    p1c = 5120 // WIDTH
    in_specs = [col(0), col(1), col(2), col(p1c), col(p1c + 1),
                halo(0), halo(1), halo(2), halo(p1c + 1),
                full((3, WIDTH)), full((8, WIDTH)), full((128, 2 * WIDTH)),
                full((128, WIDTH)), full((128, WIDTH)), full((WIDTH, WIDTH))]
    args = [proj] * 9 + [mu_rkv, vecs, wwa, g2, v2p, ones_bd]
    if has_vres:
        in_specs.append(pl.BlockSpec((ts, WIDTH), lambda i: (i, 0)))
        args.append(v_first)
    head_spec = pl.BlockSpec((1, HEADS, ts, HEAD_DIM), lambda i: (i // tps, 0, i % tps, 0))
    head_shape = jax.ShapeDtypeStruct((t // seq, HEADS, seq, HEAD_DIM), F32)
    return pl.pallas_call(
        functools.partial(_rwkv_prep_kernel, has_vres, tps),
        grid=(t // ts,),
        in_specs=in_specs,
        out_specs=[head_spec] * 6 + [pl.BlockSpec((ts, WIDTH), lambda i: (i, 0))] * 3,
        out_shape=[head_shape] * 6 + [jax.ShapeDtypeStruct((t, WIDTH), F32)] * 3,
        compiler_params=_cparams("parallel"),
    )(*args)


def _rwkv_chunk_prep_kernel(r_ref, lw_ref, k_ref, v_ref, kk_ref, b_ref,
                            atp_ref, rt_ref, arb_ref, bh_ref, w2_ref, y2_ref, z2_ref, gc_ref):
    c_len = CHUNK
    n_chunks = r_ref.shape[1] // c_len
    row = lax.broadcasted_iota(jnp.int32, (c_len, c_len), 0)
    col = lax.broadcasted_iota(jnp.int32, (c_len, c_len), 1)
    strict = row > col
    incl = row >= col
    tril = incl.astype(BF16)
    m16 = (row // 16) == (col // 16)
    m32 = (row // 32) == (col // 32)
    eye = (row == col).astype(F32)

    chunks = range(r_ref.shape[0] * n_chunks)
    sls = [(c // n_chunks, pl.ds((c % n_chunks) * c_len, c_len)) for c in chunks]
    each = lambda fn, *cols: [fn(*args) for args in zip(*cols)]
    lw = [lw_ref[sl] for sl in sls]
    v = [v_ref[sl] for sl in sls]
    cum = each(lambda x: _dot_rsplit(tril, x), lw)
    e_pos = each(jnp.exp, cum)
    e_neg = each(lambda x: jnp.exp(-x), cum)
    g_last = [x[c_len - 1:c_len, :] for x in e_pos]
    at = each(lambda sl, x, y: -kk_ref[sl] * jnp.exp(x - y), sls, cum, lw)
    rt = each(lambda sl, x: r_ref[sl] * x, sls, e_pos)
    bt = each(lambda sl, x: b_ref[sl] * x, sls, e_neg)
    kt = each(lambda sl, x: k_ref[sl] * x, sls, e_neg)
    ar = each(lambda x, y: jnp.concatenate([x, y], axis=0).astype(BF16), at, rt)
    mb = each(_dot_nt, ar, bt)
    mk = each(_dot_nt, ar, kt)
    aab = [jnp.where(strict, x[:c_len], 0.0) for x in mb]
    arb = [jnp.where(incl, x[c_len:], 0.0) for x in mb]
    aak = [jnp.where(strict, x[:c_len], 0.0) for x in mk]
    ark = [jnp.where(incl, x[c_len:], 0.0) for x in mk]
    d1 = [jnp.where(m16, x, 0.0) for x in aab]
    d2 = each(_dot, d1, d1)
    t1 = each(lambda x, y: _dot(eye + x, eye + y), d1, d2)
    d4 = each(_dot, d2, d2)
    t2 = each(lambda x, y: _dot(x, eye + y), t1, d4)
    d8 = each(_dot, d4, d4)
    t3 = each(lambda x, y: _dot(x, eye + y), t2, d8)
    x32 = each(lambda t, a: _dot(t, jnp.where(m32 & jnp.logical_not(m16), a, 0.0)), t3, aab)
    t32 = each(lambda t, x: t + _dot(x, t), t3, x32)
    x64 = each(lambda t, a: _dot(t, jnp.where(m32, 0.0, a)), t32, aab)
    tinv = each(lambda t, x: t + _dot(x, t), t32, x64)
    akv = each(_dot, aak, v)
    atp = each(_dot, tinv, at)
    w2 = each(_dot, tinv, akv)
    y2 = each(_dot, ark, v)
    z2 = each(lambda x, y, g: _dot_tn(x, y * g), v, kt, g_last)
    for c in chunks:
        sl = sls[c]
        atp_ref[sl] = atp[c].astype(BF16)
        rt_ref[sl] = rt[c].astype(BF16)
        arb_ref[sl] = arb[c].astype(BF16)
        bh_ref[sl] = (bt[c] * g_last[c]).astype(BF16)
        w2_ref[sl] = w2[c]
        y2_ref[sl] = y2[c]
        z2_ref[sl] = z2[c]
        gc_ref[c // n_chunks, c % n_chunks:c % n_chunks + 1, :] = g_last[c]


def _rwkv_scan_kernel(atp_ref, rt_ref, arb_ref, bh_ref, w2_ref, y2_ref, z2_ref, gc_ref,
                      y_ref, s_ref):
    c_len = CHUNK
    n_chunks = atp_ref.shape[1] // c_len
    group = atp_ref.shape[0]

    @pl.when(pl.program_id(1) == 0)
    def _():
        s_ref[...] = jnp.zeros_like(s_ref)

    def body(c, states):
        sl = pl.ds(pl.multiple_of(c * c_len, c_len), c_len)
        heads = range(group)
        sb = [states[gi].astype(BF16) for gi in heads]
        u = [_dot_nt(atp_ref[gi, sl, :], sb[gi]) + w2_ref[gi, sl, :] for gi in heads]
        ys = [_dot_nt(rt_ref[gi, sl, :], sb[gi]) + y2_ref[gi, sl, :] for gi in heads]
        ub = [x.astype(BF16) for x in u]
        upd = [_dot_tn(ub[gi], bh_ref[gi, sl, :]) for gi in heads]
        for gi in heads:
            y_ref[gi, sl, :] = ys[gi] + _dot(arb_ref[gi, sl, :], ub[gi])
        return tuple(states[gi] * gc_ref[gi, pl.ds(c, 1), :] + upd[gi] + z2_ref[gi, sl, :]
                     for gi in heads)

    states = lax.fori_loop(0, n_chunks, body, tuple(s_ref[gi] for gi in range(group)))
    for gi in range(group):
        s_ref[gi] = states[gi]


def _rwkv_scan(r, lw, k, v, kk, b):
    bh, s, n = r.shape
    sb = min(SCAN_BLOCK, s)
    nc = sb // CHUNK
    hp = CHUNK_PREP_HEADS
    blk = pl.BlockSpec((hp, sb, n), lambda i, j: (i, j, 0))
    gblk = pl.BlockSpec((hp, nc, n), lambda i, j: (i, j, 0))
    outs = pl.pallas_call(
        _rwkv_chunk_prep_kernel,
        grid=(bh // hp, s // sb),
        in_specs=[blk] * 6,
        out_specs=[blk] * 7 + [gblk],
        out_shape=[jax.ShapeDtypeStruct((bh, s, n), BF16)] * 4
        + [jax.ShapeDtypeStruct((bh, s, n), F32)] * 3
        + [jax.ShapeDtypeStruct((bh, s // CHUNK, n), F32)],
        compiler_params=_cparams("parallel", "parallel"),
    )(r, lw, k, v, kk, b)
    grp = SCAN_GROUP
    blk = pl.BlockSpec((grp, sb, n), lambda i, j: (i, j, 0))
    gblk = pl.BlockSpec((grp, nc, n), lambda i, j: (i, j, 0))
    return pl.pallas_call(
        _rwkv_scan_kernel,
        grid=(bh // grp, s // sb),
        in_specs=[blk] * 7 + [gblk],
        out_specs=blk,
        out_shape=jax.ShapeDtypeStruct((bh, s, n), F32),
        scratch_shapes=[pltpu.VMEM((grp, n, n), F32)],
        compiler_params=_cparams("parallel", "arbitrary"),
    )(*outs)


def _moba_prep_kernel(q_ref, k_ref, v_ref, gain_ref, ones_ref, qo_ref, qt_ref, ko_ref, vt_ref, km_ref):
    ones_bd = ones_ref[...]
    q = q_ref[...]
    k = k_ref[...]
    inv_d = 1.0 / HEAD_DIM
    qn = q * lax.rsqrt(_dot_lsplit(q * q, ones_bd) * inv_d + QK_EPS) * gain_ref[0:1, :]
    kn = k * lax.rsqrt(_dot_lsplit(k * k, ones_bd) * inv_d + QK_EPS) * gain_ref[1:2, :]
    qs = qn * (HEAD_DIM ** -0.5 * LOG2E)
    qo_ref[...] = qs
    qt_ref[0] = qs.T.astype(BF16)
    _store_heads(ko_ref, kn)
    vt_ref[0] = v_ref[...].T.astype(BF16)
    km_ref[0] = jnp.mean(kn, axis=0, keepdims=True)


def _moba_prep(proj, seq, gains, ones_bd):
    t = proj.shape[0]
    ts = MOBA_BLOCK
    tps = seq // ts
    col = lambda c: pl.BlockSpec((ts, WIDTH), lambda i, c=c: (i, c))
    tspec = pl.BlockSpec((1, WIDTH, ts), lambda i: (i // tps, 0, i % tps))
    tshape = jax.ShapeDtypeStruct((t // seq, WIDTH, seq), BF16)
    return pl.pallas_call(
        _moba_prep_kernel,
        grid=(t // ts,),
        in_specs=[col(3), col(4), col(5),
                  pl.BlockSpec((2, WIDTH), lambda i: (0, 0)),
                  pl.BlockSpec((WIDTH, WIDTH), lambda i: (0, 0))],
        out_specs=[pl.BlockSpec((ts, WIDTH), lambda i: (i, 0)), tspec,
                   pl.BlockSpec((1, HEADS, ts, HEAD_DIM), lambda i: (i // tps, 0, i % tps, 0)),
                   tspec, pl.BlockSpec((1, 1, WIDTH), lambda i: (i, 0, 0))],
        out_shape=[jax.ShapeDtypeStruct((t, WIDTH), F32), tshape,
                   jax.ShapeDtypeStruct((t // seq, HEADS, seq, HEAD_DIM), BF16),
                   tshape, jax.ShapeDtypeStruct((t // ts, 1, WIDTH), F32)],
        compiler_params=_cparams("parallel"),
    )(proj, proj, proj, gains, ones_bd)


def _gating_kernel(q_ref, km_ref, o_ref):
    ah, al = _split(km_ref[0])
    bh, bl = _split(q_ref[...])
    nt = lambda a, b: lax.dot_general(a, b, (((1,), (1,)), ((), ())), preferred_element_type=F32)
    o_ref[0] = nt(ah, bh) + nt(al, bh) + nt(ah, bl)


def _gating_scores(qf, km_t, batch):
    t = qf.shape[0]
    ts = ROW_TILE
    per = t // batch // ts
    rows = km_t.shape[1]
    return pl.pallas_call(
        _gating_kernel,
        grid=(batch, per),
        in_specs=[pl.BlockSpec((ts, WIDTH), lambda b, i: (b * per + i, 0)),
                  pl.BlockSpec((1, rows, WIDTH), lambda b, i: (b, 0, 0))],
        out_specs=pl.BlockSpec((1, rows, ts), lambda b, i: (b, 0, i)),
        out_shape=jax.ShapeDtypeStruct((batch, rows, t // batch), F32),
        compiler_params=_cparams("parallel", "parallel"),
    )(qf, km_t)


def _rel_bias_kernel(tab_ref, o_ref):
    h = pl.program_id(0)
    blk = MOBA_BLOCK
    key = lax.broadcasted_iota(jnp.int32, (blk, blk), 0)
    qry = lax.broadcasted_iota(jnp.int32, (blk, blk), 1)
    max_exact = REL_BUCKETS // 2

    def lookup(n):
        nf = jnp.maximum(n, 1).astype(F32)
        large = max_exact + (jnp.log(nf / max_exact) / math.log(REL_MAX_DIST / max_exact)
                             * (REL_BUCKETS - max_exact)).astype(jnp.int32)
        bucket = jnp.where(n < max_exact, n, jnp.minimum(large, REL_BUCKETS - 1))
        out = jnp.zeros((blk, blk), F32)
        for bkt in range(REL_BUCKETS):
            out = jnp.where(bucket == bkt, tab_ref[bkt, h], out)
        return out

    dist = qry - key
    far = lookup(dist + 2 * blk)
    o_ref[0, 0] = jnp.where(dist >= 0, (lookup(jnp.maximum(dist, 0)) - far) * LOG2E, NEG_BIG)
    o_ref[0, 1] = (lookup(dist + blk) - far) * LOG2E


def _rel_bias_tiles(rel_bias):
    return pl.pallas_call(
        _rel_bias_kernel,
        grid=(HEADS,),
        in_specs=[pl.BlockSpec(memory_space=pltpu.SMEM)],
        out_specs=pl.BlockSpec((1, 2, MOBA_BLOCK, MOBA_BLOCK), lambda h: (h, 0, 0, 0)),
        out_shape=jax.ShapeDtypeStruct((HEADS, 2, MOBA_BLOCK, MOBA_BLOCK), F32),
        compiler_params=_cparams("parallel"),
    )(rel_bias)


def _moba_attn_kernel(qt_ref, k_ref, vt_ref, sc_ref, bias_ref, o_ref, mask_ref, sa_ref, sb_ref,
                      pend_ref):
    blk = MOBA_BLOCK
    i = pl.program_id(1)
    nb = sc_ref.shape[1]
    heads = range(qt_ref.shape[0])
    dh = vt_ref.shape[1]
    bidx = lax.broadcasted_iota(jnp.int32, (nb, blk), 0)
    past = bidx < i
    ones_rows = jnp.ones((SUM_ROWS, blk), BF16)

    def block(j):
        return pl.ds(pl.multiple_of(j * blk, blk), blk)

    def scores(hh, j):
        return jnp.dot(k_ref[hh, block(j), :], qt_ref[hh], preferred_element_type=F32)

    def weighted_values(hh, j, p):
        vals = jnp.concatenate([vt_ref[hh, :, block(j)], ones_rows], axis=0)
        return jnp.dot(vals, p, preferred_element_type=F32)

    def softmax_step(s, m, mrow):
        m_new = jnp.maximum(m, jnp.max(s, axis=0, keepdims=True) + mrow)
        return m_new, jnp.exp2(m - m_new), jnp.exp2((s - (m_new - mrow)).astype(BF16))

    j_prev = jnp.maximum(i - 1, 0)
    s_own = [scores(hh, i) + bias_ref[hh, 0] for hh in heads]
    s_prev = [scores(hh, j_prev) + bias_ref[hh, 1] for hh in heads]
    n_far = jnp.maximum(i - 1, 0)
    last_far = jnp.maximum(n_far - 1, 0)

    def store_scores(s_ref, j):
        for hh in heads:
            s_ref[hh] = scores(hh, jnp.minimum(j, last_far))

    store_scores(sa_ref, 0)

    for hh in heads:
        scm = jnp.where(past, sc_ref[hh], -jnp.inf)
        sel = jnp.zeros((nb, blk), jnp.bool_)
        for _ in range(MOBA_TOPK):
            top = jnp.max(scm, axis=0, keepdims=True)
            idx = jnp.min(jnp.where(scm == top, bidx, nb), axis=0, keepdims=True)
            pick = bidx == idx
            sel = jnp.logical_or(sel, pick)
            scm = jnp.where(pick, -jnp.inf, scm)
        mask_ref[hh] = jnp.where(jnp.logical_and(sel, past), 0.0, NEG_BIG)

    ms, accs, pend = [], [], []
    for hh in heads:
        m = jnp.max(s_own[hh], axis=0, keepdims=True)
        accs.append(weighted_values(hh, i, jnp.exp2((s_own[hh] - m).astype(BF16))))
        ms.append(m)
    for hh in heads:
        mrow = jnp.where(i > 0, mask_ref[hh, pl.ds(j_prev, 1), :], NEG_BIG)
        ms[hh], alpha, pend_ref[hh] = softmax_step(s_prev[hh], ms[hh], mrow)
        pend.append(alpha)

    def far_body(t, carry):
        j_pend, ms, accs, pend = carry
        j0 = 2 * t
        j1 = jnp.minimum(j0 + 1, last_far)
        pv_pend = [weighted_values(hh, j_pend, pend_ref[hh]) for hh in heads]
        store_scores(sb_ref, j0 + 1)
        steps = [softmax_step(sa_ref[hh], ms[hh], mask_ref[hh, pl.ds(j0, 1), :]) for hh in heads]
        accs = [pend[hh] * accs[hh] + pv_pend[hh] for hh in heads]
        pv0 = [weighted_values(hh, j0, steps[hh][2]) for hh in heads]
        store_scores(sa_ref, j0 + 2)
        new_ms, new_pend = [], []
        for hh in heads:
            mrow = jnp.where(j0 + 1 < n_far, mask_ref[hh, pl.ds(j1, 1), :], NEG_BIG)
            m, alpha, pend_ref[hh] = softmax_step(sb_ref[hh], steps[hh][0], mrow)
            new_ms.append(m)
            new_pend.append(alpha)
        accs = [steps[hh][1] * accs[hh] + pv0[hh] for hh in heads]
        return j1, tuple(new_ms), tuple(accs), tuple(new_pend)

    j_pend, ms, accs, pend = lax.fori_loop(
        0, (n_far + 1) // 2, far_body, (j_prev, tuple(ms), tuple(accs), tuple(pend)))
    for hh in heads:
        acc = pend[hh] * accs[hh] + weighted_values(hh, j_pend, pend_ref[hh])
        o_ref[hh] = acc[:dh] / acc[dh:dh + 1]


def _moba_attn(qt, k, vt, sc_t, bias):
    bh, s, dh = k.shape
    nb = sc_t.shape[1]
    blk = MOBA_BLOCK
    hp = ATTN_HEADS_PER_STEP
    return pl.pallas_call(
        _moba_attn_kernel,
        grid=(bh // hp, s // blk),
        in_specs=[pl.BlockSpec((hp, dh, blk), lambda b, i: (b, 0, i)),
                  pl.BlockSpec((hp, s, dh), lambda b, i: (b, 0, 0)),
                  pl.BlockSpec((hp, dh, s), lambda b, i: (b, 0, 0)),
                  pl.BlockSpec((hp, nb, blk), lambda b, i: (b, 0, i)),
                  pl.BlockSpec((hp, 2, blk, blk), lambda b, i: (b % (HEADS // hp), 0, 0, 0))],
        out_specs=pl.BlockSpec((hp, dh, blk), lambda b, i: (b, 0, i)),
        out_shape=jax.ShapeDtypeStruct((bh, dh, s), F32),
        scratch_shapes=[pltpu.VMEM((hp, nb, blk), F32), pltpu.VMEM((hp, blk, blk), F32),
                        pltpu.VMEM((hp, blk, blk), F32), pltpu.VMEM((hp, blk, blk), BF16)],
        compiler_params=_cparams("parallel", "parallel"),
    )(qt, k, vt, sc_t, bias)


def _merge_router_kernel(x_ref, ys_ref, bonus_ref, g_ref, ym_ref, gr_ref, gm_ref,
                         lnx_ref, ones_ref, wr_ref, wm_ref, wo_ref, gf_ref, wrt_ref, brt_ref,
                         xo_ref, xn_ref, route_ref):
    ones_bd = ones_ref[...]
    inv_n = 1.0 / HEAD_DIM
    y = _load_heads(ys_ref)
    mean = _dot_lsplit(y, ones_bd) * inv_n
    d = y - mean
    var = _dot_lsplit(d * d, ones_bd) * inv_n
    yr = d * lax.rsqrt(var + LN_X_EPS) * lnx_ref[0:1, :] + lnx_ref[1:2, :]
    yr = (yr + bonus_ref[...]) * g_ref[...]
    h = (_sigmoid(gr_ref[...]) * _dot(yr, wr_ref[...])
         + _sigmoid(gm_ref[...]) * _dot(ym_ref[0].T, wm_ref[...]))
    x = x_ref[...] + _dot(h, wo_ref[...])
    xo_ref[...] = x

    ms = jnp.mean(x * x, axis=-1, keepdims=True)
    xn = x * lax.rsqrt(ms + NORM_EPS) * gf_ref[...]
    xn_ref[...] = xn

    logits = _dot_x3(xn, wrt_ref[...]) + brt_ref[...]
    lane = lax.broadcasted_iota(jnp.int32, logits.shape, 1)
    big = logits.shape[1]
    is_grp = lane < N_GROUPS
    lg = jnp.where(is_grp, logits, -jnp.inf)
    mg = jnp.max(lg, axis=1, keepdims=True)
    g_sel = jnp.min(jnp.where(lg == mg, lane, big), axis=1, keepdims=True)
    p_top = 1.0 / jnp.sum(jnp.where(is_grp, jnp.exp(logits - mg), 0.0), axis=1, keepdims=True)
    lo_lane = N_GROUPS + g_sel * EXPERTS_PER_GROUP
    in_grp = jnp.logical_and(lane >= lo_lane, lane < lo_lane + EXPERTS_PER_GROUP)
    le = jnp.where(in_grp, logits, -jnp.inf)
    m1 = jnp.max(le, axis=1, keepdims=True)
    i1 = jnp.min(jnp.where(le == m1, lane, big), axis=1, keepdims=True)
    le2 = jnp.where(lane == i1, -jnp.inf, le)
    m2 = jnp.max(le2, axis=1, keepdims=True)
    i2 = jnp.min(jnp.where(le2 == m2, lane, big), axis=1, keepdims=True)
    e2 = jnp.exp(m2 - m1)
    gate1 = p_top / (1.0 + e2)
    gate2 = p_top * e2 / (1.0 + e2)
    route = jnp.where(lane == 0, (i1 - N_GROUPS).astype(F32),
                      jnp.where(lane == 1, (i2 - N_GROUPS).astype(F32),
                                jnp.where(lane == 2, gate1, jnp.where(lane == 3, gate2, 0.0))))
    route_ref[...] = route


def _merge_router(x, ys, bonus, g, ym_t, proj, lnx, ones_bd, wr, wm, wo, gf, wrt, brt):
    t, d = x.shape
    tm = MERGE_TILE
    tps = ys.shape[2] // tm
    row = lambda w: pl.BlockSpec((tm, w), lambda i: (i, 0))
    full = lambda shape: pl.BlockSpec(shape, lambda i: tuple(0 for _ in shape))
    return pl.pallas_call(
        _merge_router_kernel,
        grid=(t // tm,),
        in_specs=[row(d),
                  pl.BlockSpec((1, HEADS, tm, HEAD_DIM), lambda i: (i // tps, 0, i % tps, 0)),
                  row(WIDTH), row(WIDTH),
                  pl.BlockSpec((1, WIDTH, tm), lambda i: (i // tps, 0, i % tps)),
                  pl.BlockSpec((tm, d), lambda i: (i, 3)),
                  pl.BlockSpec((tm, d), lambda i: (i, 4)),
                  full((2, WIDTH)), full((WIDTH, WIDTH)), full((WIDTH, d)), full((WIDTH, d)),
                  full((d, d)), full((1, d)), full((d, 128)), full((1, 128))],
        out_specs=[row(d), row(d), row(128)],
        out_shape=[jax.ShapeDtypeStruct((t, d), F32), jax.ShapeDtypeStruct((t, d), F32),
                   jax.ShapeDtypeStruct((t, 128), F32)],
        compiler_params=_cparams("parallel"),
    )(x, ys, bonus, g, ym_t, proj, proj, lnx, ones_bd, wr, wm, wo, gf, wrt, brt)


def _moe_dispatch_kernel(pos_ref, x_ref, xs_in, xs_out, sem):
    del xs_in
    i = pl.program_id(0)
    tm = x_ref.shape[0]

    def row_copy(r, j):
        p = pos_ref[(i * tm + r) * 2 + j]
        return pltpu.make_async_copy(x_ref.at[pl.ds(r, 1), :], xs_out.at[pl.ds(p, 1), :], sem)

    def start(r, c):
        row_copy(r, 0).start()
        row_copy(r, 1).start()
        return c

    def wait(r, c):
        row_copy(r, 0).wait()
        row_copy(r, 1).wait()
        return c

    lax.fori_loop(0, tm, start, 0, unroll=DMA_UNROLL)
    lax.fori_loop(0, tm, wait, 0, unroll=DMA_UNROLL)


def _moe_dispatch(pos, xn, n_rows):
    t, d = xn.shape
    tm = ROW_TILE
    grid_spec = pltpu.PrefetchScalarGridSpec(
        num_scalar_prefetch=1,
        grid=(t // tm,),
        in_specs=[pl.BlockSpec((tm, d), lambda i, p: (i, 0)),
                  pl.BlockSpec(memory_space=pl.ANY)],
        out_specs=pl.BlockSpec(memory_space=pl.ANY),
        scratch_shapes=[pltpu.SemaphoreType.DMA(())],
    )
    return pl.pallas_call(
        _moe_dispatch_kernel,
        grid_spec=grid_spec,
        out_shape=jax.ShapeDtypeStruct((n_rows, d), F32),
        input_output_aliases={2: 0},
        compiler_params=_cparams("arbitrary"),
    )(pos, xn, jnp.zeros((n_rows, d), F32))


def _moe_ffn_kernel(texp_ref, nused_ref, x_ref, wg_ref, wu_ref, wd_ref, o_ref, wg_bf, wu_bf, wd_bf):
    i = pl.program_id(0)
    used = i < nused_ref[0]
    new_expert = jnp.logical_or(i == 0, texp_ref[i] != texp_ref[jnp.maximum(i - 1, 0)])

    @pl.when(jnp.logical_and(used, new_expert))
    def _():
        wg_bf[...] = wg_ref[0, 0].astype(BF16)
        wu_bf[...] = wu_ref[0, 0].astype(BF16)
        wd_bf[...] = wd_ref[0, 0].astype(BF16)

    @pl.when(used)
    def _():
        xb = x_ref[...].astype(BF16)
        hg = jnp.dot(xb, wg_bf[...], preferred_element_type=F32)
        hu = jnp.dot(xb, wu_bf[...], preferred_element_type=F32)
        h = hg * _sigmoid(hg) * hu
        o_ref[...] = jnp.dot(h.astype(BF16), wd_bf[...], preferred_element_type=F32)

    @pl.when(jnp.logical_not(used))
    def _():
        o_ref[...] = jnp.zeros_like(o_ref)


def _moe_ffn(tile_expert, nused, xs, wg, wu, wd, layer):
    n_rows, d = xs.shape
    tm = MOE_TILE
    dh = wg.shape[3]
    grid_spec = pltpu.PrefetchScalarGridSpec(
        num_scalar_prefetch=2,
        grid=(n_rows // tm,),
        in_specs=[pl.BlockSpec((tm, d), lambda i, te, nu: (jnp.minimum(i, nu[0] - 1), 0)),
                  pl.BlockSpec((1, 1, d, dh), lambda i, te, nu: (layer, te[i], 0, 0)),
                  pl.BlockSpec((1, 1, d, dh), lambda i, te, nu: (layer, te[i], 0, 0)),
                  pl.BlockSpec((1, 1, dh, d), lambda i, te, nu: (layer, te[i], 0, 0))],
        out_specs=pl.BlockSpec((tm, d), lambda i, te, nu: (i, 0)),
        scratch_shapes=[pltpu.VMEM((d, dh), BF16), pltpu.VMEM((d, dh), BF16),
                        pltpu.VMEM((dh, d), BF16)],
    )
    return pl.pallas_call(
        _moe_ffn_kernel,
        grid_spec=grid_spec,
        out_shape=jax.ShapeDtypeStruct((n_rows, d), F32),
        compiler_params=_cparams("arbitrary"),
    )(tile_expert, nused, xs, wg, wu, wd)


def _moe_combine_kernel(pos_ref, ys_hbm, x_ref, route_ref, o_ref, buf0, buf1, sem):
    i = pl.program_id(0)
    tm = buf0.shape[0]

    def row_copy(r, j, buf):
        p = pos_ref[(i * tm + r) * 2 + j]
        return pltpu.make_async_copy(ys_hbm.at[pl.ds(p, 1), :], buf.at[pl.ds(r, 1), :], sem)

    def start(r, c):
        row_copy(r, 0, buf0).start()
        row_copy(r, 1, buf1).start()
        return c

    def wait(r, c):
        row_copy(r, 0, buf0).wait()
        row_copy(r, 1, buf1).wait()
        return c

    lax.fori_loop(0, tm, start, 0, unroll=DMA_UNROLL)
    lax.fori_loop(0, tm, wait, 0, unroll=DMA_UNROLL)
    route = route_ref[...]
    o_ref[...] = x_ref[...] + route[:, 2:3] * buf0[...] + route[:, 3:4] * buf1[...]


def _moe_combine(pos, ys, x, route):
    t, d = x.shape
    tm = ROW_TILE
    grid_spec = pltpu.PrefetchScalarGridSpec(
        num_scalar_prefetch=1,
        grid=(t // tm,),
        in_specs=[pl.BlockSpec(memory_space=pl.ANY),
                  pl.BlockSpec((tm, d), lambda i, p: (i, 0)),
                  pl.BlockSpec((tm, 128), lambda i, p: (i, 0))],
        out_specs=pl.BlockSpec((tm, d), lambda i, p: (i, 0)),
        scratch_shapes=[pltpu.VMEM((tm, d), F32), pltpu.VMEM((tm, d), F32),
                        pltpu.SemaphoreType.DMA(())],
    )
    return pl.pallas_call(
        _moe_combine_kernel,
        grid_spec=grid_spec,
        out_shape=jax.ShapeDtypeStruct((t, d), F32),
        compiler_params=_cparams("arbitrary"),
    )(pos, ys, x, route)


def _moe_plan(route):
    tm = MOE_TILE
    e_flat = route[:, 0:2].astype(jnp.int32).reshape(-1)
    n_rows = e_flat.shape[0] + N_EXPERTS * tm
    onehot = (e_flat[:, None] == jnp.arange(N_EXPERTS, dtype=jnp.int32)[None, :]).astype(jnp.int32)
    running = jnp.cumsum(onehot, axis=0)
    counts = running[-1]
    padded = ((counts + tm - 1) // tm) * tm
    pend = jnp.cumsum(padded)
    pstart = pend - padded
    pos = jnp.sum(onehot * (running - 1 + pstart[None, :]), axis=1).astype(jnp.int32)
    tile_start = jnp.arange(n_rows // tm, dtype=jnp.int32) * tm
    tile_expert = jnp.minimum(jnp.sum(pend[None, :] <= tile_start[:, None], axis=1),
                              N_EXPERTS - 1).astype(jnp.int32)
    nused = (pend[-1:] // tm).astype(jnp.int32)
    return tile_expert, nused, pos, n_rows


def _pad_cols(w, width):
    return jnp.pad(w, ((0, 0), (0, width - w.shape[1])))


def _pad_rows(w, rows):
    return jnp.pad(w, ((0, rows - w.shape[0]), (0, 0)))


def kernel(x, norm_mix_g, norm_ffn_g, w_in, mu_rkv, mu_x, w0, w1, w2, a0, a1, a2, g1, g2, k_k, k_a, r_k, lnx_g, lnx_b, vres_mu, vres_v0, vres_v1, vres_v2, q_gain, k_gain, rel_bias, w_br_rwkv, w_br_moba, w_out, w_grp, b_grp, w_exp, b_exp, w_gate, w_up, w_down):
    batch, seq, d_model = x.shape
    depth = w_in.shape[0]
    t = batch * seq
    nb = seq // MOBA_BLOCK
    assert seq % MOBA_BLOCK == 0 and seq % SCAN_BLOCK == 0 and (batch * HEADS) % SCAN_GROUP == 0

    head_of = jnp.arange(WIDTH) // HEAD_DIM
    ones_bd = (head_of[:, None] == head_of[None, :]).astype(BF16)
    bias_tiles = _rel_bias_tiles(rel_bias)
    gains = jnp.stack([jnp.tile(q_gain, (1, HEADS)), jnp.tile(k_gain, (1, HEADS))], axis=1)

    xf = x.reshape(t, d_model)
    v_first = None
    for l in range(depth):
        has_vres = l > 0
        lora_w = [w1[l], a1[l], g1[l]]
        lora_mu = [mu_x[l, 0], mu_x[l, 1], mu_x[l, 2]]
        if has_vres:
            lora_w.append(vres_v1[l - 1])
            lora_mu.append(vres_mu[l - 1])
        p1 = _pad_cols(jnp.concatenate([w * (1.0 - m)[:, None] for w, m in zip(lora_w, lora_mu)], 1),
                       LORA_PAD)
        p2 = _pad_cols(jnp.concatenate([w * m[:, None] for w, m in zip(lora_w, lora_mu)], 1), LORA_PAD)
        w_all = jnp.concatenate([w_in[l], p1, p2], axis=1).astype(BF16)
        wwa = jnp.concatenate([
            jnp.concatenate([w2[l], jnp.zeros((DECAY_LORA, WIDTH), F32)], 1),
            jnp.concatenate([jnp.zeros((AAA_LORA, WIDTH), F32), a2[l]], 1)], 0).astype(BF16)
        v2p = (_pad_rows(vres_v2[l - 1], 128) if has_vres else jnp.zeros((128, WIDTH), F32)).astype(BF16)
        v0 = vres_v0[l - 1] if has_vres else jnp.zeros((WIDTH,), F32)
        vecs = jnp.stack([w0[l], a0[l], k_k[l], k_a[l], r_k[l].reshape(-1), v0,
                          jnp.zeros((WIDTH,), F32), jnp.zeros((WIDTH,), F32)])

        proj = _norm_proj(xf, norm_mix_g[l], w_all)

        *heads, v_row, g, bonus = _rwkv_prep(proj, seq, mu_rkv[l], vecs, wwa,
                                             g2[l].astype(BF16), v2p, ones_bd, v_first)
        if l == 0:
            v_first = v_row
        bh = batch * HEADS
        y_scan = _rwkv_scan(*[a_.reshape(bh, seq, HEAD_DIM) for a_ in heads]).reshape(
            batch, HEADS, seq, HEAD_DIM)

        qf, q_t, k_h, v_t, kmean = _moba_prep(proj, seq, gains[l], ones_bd)
        km = kmean.reshape(batch, nb, HEADS, HEAD_DIM)
        km_t = jnp.einsum("bjhd,hg->bgjhd", km, jnp.eye(HEADS, dtype=F32)).reshape(
            batch, HEADS * nb, WIDTH)
        sc_t = _gating_scores(qf, km_t, batch).reshape(bh, nb, seq)
        y_m = _moba_attn(q_t.reshape(bh, HEAD_DIM, seq), k_h.reshape(bh, seq, HEAD_DIM),
                         v_t.reshape(bh, HEAD_DIM, seq), sc_t, bias_tiles).reshape(
                             batch, WIDTH, seq)

        wrt = _pad_cols(jnp.concatenate([w_grp[l], w_exp[l]], axis=1), 128)
        brt = _pad_cols(jnp.concatenate([b_grp[l], b_exp[l]])[None, :], 128)
        lnx = jnp.stack([lnx_g[l], lnx_b[l]])
        xf, xn2, route = _merge_router(
            xf, y_scan, bonus, g, y_m, proj, lnx, ones_bd,
            w_br_rwkv[l].astype(BF16), w_br_moba[l].astype(BF16), w_out[l].astype(BF16),
            norm_ffn_g[l].reshape(1, d_model), wrt, brt)

        tile_expert, nused, pos, n_rows = _moe_plan(route)
        xs = _moe_dispatch(pos, xn2, n_rows)
        ys = _moe_ffn(tile_expert, nused, xs, w_gate, w_up, w_down, l)
        xf = _moe_combine(pos, ys, xf, route)
    return xf.reshape(batch, seq, d_model)
```

```python
import functools
import math

import jax
import jax.numpy as jnp
from jax import lax
from jax.experimental import pallas as pl
from jax.experimental.pallas import tpu as pltpu

F32 = jnp.float32
BF16 = jnp.bfloat16

HEADS = 8
HEAD_DIM = 64
WIDTH = HEADS * HEAD_DIM
DECAY_LORA, AAA_LORA, MV_LORA, GATE_LORA = 64, 64, 32, 128
LN_X_EPS = 64e-5
MOBA_BLOCK = 256
MOBA_TOPK = 3
REL_BUCKETS = 32
REL_MAX_DIST = 128
N_GROUPS = 4
EXPERTS_PER_GROUP = 8
N_EXPERTS = N_GROUPS * EXPERTS_PER_GROUP
NORM_EPS = 1e-6
QK_EPS = 1e-6

CHUNK = 64
SCAN_BLOCK = 512
SCAN_GROUP = 8
CHUNK_PREP_HEADS = 4
LORA_PAD = 512
ROW_TILE = 256
MOE_TILE = 512
MOE_ROW_TILE = 512
MERGE_TILE = 512
DMA_UNROLL = 8
NEG_BIG = -1e30
LOG2E = math.log2(math.e)
ATTN_HEADS_PER_STEP = 4
SUM_ROWS = 16
VMEM_LIMIT = 48 * 1024 * 1024


def _cparams(*sem):
    return pltpu.CompilerParams(dimension_semantics=sem, vmem_limit_bytes=VMEM_LIMIT)


def _dot(a, b):
    return jnp.dot(a.astype(BF16), b.astype(BF16), preferred_element_type=F32)


def _dot_nt(a, b):
    return lax.dot_general(a.astype(BF16), b.astype(BF16), (((1,), (1,)), ((), ())),
                           preferred_element_type=F32)


def _dot_tn(a, b):
    return lax.dot_general(a.astype(BF16), b.astype(BF16), (((0,), (0,)), ((), ())),
                           preferred_element_type=F32)


def _split(x):
    hi = x.astype(BF16)
    lo = (x - hi.astype(F32)).astype(BF16)
    return hi, lo


def _dot_lsplit(a, b_exact):
    hi, lo = _split(a)
    return (jnp.dot(hi, b_exact, preferred_element_type=F32)
            + jnp.dot(lo, b_exact, preferred_element_type=F32))


def _dot_rsplit(a_exact, b):
    hi, lo = _split(b)
    return (jnp.dot(a_exact, hi, preferred_element_type=F32)
            + jnp.dot(a_exact, lo, preferred_element_type=F32))


def _dot_x3(a, b):
    ah, al = _split(a)
    bh, bl = _split(b)
    return (jnp.dot(ah, bh, preferred_element_type=F32)
            + jnp.dot(al, bh, preferred_element_type=F32)
            + jnp.dot(ah, bl, preferred_element_type=F32))


def _sigmoid(x):
    return 1.0 / (1.0 + jnp.exp(-x))


def _norm_proj_kernel(x_ref, g_ref, w_ref, o_ref, xn_ref):
    @pl.when(pl.program_id(1) == 0)
    def _():
        x = x_ref[...]
        ms = jnp.mean(x * x, axis=-1, keepdims=True)
        xn_ref[...] = (x * lax.rsqrt(ms + NORM_EPS) * g_ref[...]).astype(BF16)

    o_ref[...] = jnp.dot(xn_ref[...], w_ref[...], preferred_element_type=F32)


def _norm_proj(x, g, w, tm=1024, tn=2048):
    m, k = x.shape
    n = w.shape[1]
    return pl.pallas_call(
        _norm_proj_kernel,
        grid=(m // tm, n // tn),
        in_specs=[pl.BlockSpec((tm, k), lambda i, j: (i, 0)),
                  pl.BlockSpec((1, k), lambda i, j: (0, 0)),
                  pl.BlockSpec((k, tn), lambda i, j: (0, j))],
        out_specs=pl.BlockSpec((tm, tn), lambda i, j: (i, j)),
        out_shape=jax.ShapeDtypeStruct((m, n), F32),
        scratch_shapes=[pltpu.VMEM((tm, k), BF16)],
        compiler_params=_cparams("parallel", "arbitrary"),
    )(x, g.reshape(1, k), w)


def _store_heads(o_ref, val):
    for h in range(HEADS):
        o_ref[0, h] = val[:, h * HEAD_DIM:(h + 1) * HEAD_DIM].astype(o_ref.dtype)


def _load_heads(ref):
    return jnp.concatenate([ref[0, h] for h in range(HEADS)], axis=1)


def _shift_rows(x, halo_ref, first):
    prev = jnp.where(first, 0.0, halo_ref[7:8, :])
    rolled = pltpu.roll(x, 1, axis=0)
    row = lax.broadcasted_iota(jnp.int32, x.shape, 0)
    return jnp.where(row == 0, prev, rolled)


def _rwkv_prep_kernel(has_vres, tiles_per_seq, *refs):
    (rp_ref, kp_ref, vp_ref, p1_ref, p2_ref, rph_ref, kph_ref, vph_ref, p2h_ref,
     mu_ref, vec_ref, wwa_ref, g2_ref, v2_ref, ones_ref) = refs[:15]
    refs = refs[15:]
    if has_vres:
        vfirst_ref = refs[0]
        refs = refs[1:]
    r_o, lw_o, k_o, v_o, kk_o, b_o, vrow_o, g_o, bonus_o = refs

    first = (pl.program_id(0) % tiles_per_seq) == 0
    rp, kp, vp = rp_ref[...], kp_ref[...], vp_ref[...]
    r = rp + (_shift_rows(rp, rph_ref, first) - rp) * mu_ref[0:1, :]
    k = kp + (_shift_rows(kp, kph_ref, first) - kp) * mu_ref[1:2, :]
    v = vp + (_shift_rows(vp, vph_ref, first) - vp) * mu_ref[2:3, :]
    lo = p1_ref[...] + _shift_rows(p2_ref[...], p2h_ref, first)

    w0, a0, k_k, k_a = vec_ref[0:1, :], vec_ref[1:2, :], vec_ref[2:3, :], vec_ref[3:4, :]
    r_k, v0 = vec_ref[4:5, :], vec_ref[5:6, :]

    h_wa = lo[:, 0:128]
    lane = lax.broadcasted_iota(jnp.int32, h_wa.shape, 1)
    h_wa = jnp.where(lane < DECAY_LORA, jnp.tanh(h_wa), h_wa)
    wa = _dot(h_wa, wwa_ref[...])
    z = -(w0 + wa[:, :WIDTH])
    softplus = jnp.maximum(z, 0.0) + jnp.log(1.0 + jnp.exp(-jnp.abs(z)))
    logw = -jnp.exp(-softplus - 0.5)
    a = _sigmoid(a0 + wa[:, WIDTH:])
    g = _dot(_sigmoid(lo[:, 128:256]), g2_ref[...])
    if has_vres:
        gate_v = _sigmoid(v0 + _dot(lo[:, 256:384], v2_ref[...]))
        v = v + (vfirst_ref[...] - v) * gate_v

    ones_bd = ones_ref[...]
    kk = k * k_k
    ss = _dot_lsplit(kk * kk, ones_bd)
    kk = kk / jnp.maximum(jnp.sqrt(ss), 1e-12)
    k = k * (1.0 + (a - 1.0) * k_a)
    bonus = _dot_lsplit(r * k * r_k, ones_bd) * v

    _store_heads(r_o, r)
    _store_heads(lw_o, logw)
    _store_heads(k_o, k)
    _store_heads(v_o, v)
    _store_heads(kk_o, kk)
    _store_heads(b_o, kk * a)
    vrow_o[...] = v
    g_o[...] = g
    bonus_o[...] = bonus


def _rwkv_prep(proj, seq, mu_rkv, vecs, wwa, g2, v2p, ones_bd, v_first):
    t = proj.shape[0]
    ts = ROW_TILE
    tps = seq // ts
    has_vres = v_first is not None
    col = lambda c: pl.BlockSpec((ts, WIDTH), lambda i, c=c: (i, c))
    halo = lambda c: pl.BlockSpec((8, WIDTH), lambda i, c=c: (jnp.maximum(i * (ts // 8) - 1, 0), c))
    full = lambda shape: pl.BlockSpec(shape, lambda i: tuple(0 for _ in shape))
    p1c = 5120 // WIDTH
    in_specs = [col(0), col(1), col(2), col(p1c), col(p1c + 1),
                halo(0), halo(1), halo(2), halo(p1c + 1),
                full((3, WIDTH)), full((8, WIDTH)), full((128, 2 * WIDTH)),
                full((128, WIDTH)), full((128, WIDTH)), full((WIDTH, WIDTH))]
    args = [proj] * 9 + [mu_rkv, vecs, wwa, g2, v2p, ones_bd]
    if has_vres:
        in_specs.append(pl.BlockSpec((ts, WIDTH), lambda i: (i, 0)))
        args.append(v_first)
    head_spec = pl.BlockSpec((1, HEADS, ts, HEAD_DIM), lambda i: (i // tps, 0, i % tps, 0))
    head_shape = jax.ShapeDtypeStruct((t // seq, HEADS, seq, HEAD_DIM), F32)
    return pl.pallas_call(
        functools.partial(_rwkv_prep_kernel, has_vres, tps),
        grid=(t // ts,),
        in_specs=in_specs,
        out_specs=[head_spec] * 6 + [pl.BlockSpec((ts, WIDTH), lambda i: (i, 0))] * 3,
        out_shape=[head_shape] * 6 + [jax.ShapeDtypeStruct((t, WIDTH), F32)] * 3,
        compiler_params=_cparams("parallel"),
    )(*args)


def _rwkv_chunk_prep_kernel(r_ref, lw_ref, k_ref, v_ref, kk_ref, b_ref,
                            atp_ref, rt_ref, arb_ref, bh_ref, w2_ref, y2_ref, z2_ref, gc_ref):
    c_len = CHUNK
    n_chunks = r_ref.shape[1] // c_len
    row = lax.broadcasted_iota(jnp.int32, (c_len, c_len), 0)
    col = lax.broadcasted_iota(jnp.int32, (c_len, c_len), 1)
    strict = row > col
    incl = row >= col
    tril = incl.astype(BF16)
    m16 = (row // 16) == (col // 16)
    m32 = (row // 32) == (col // 32)
    eye = (row == col).astype(F32)

    chunks = range(r_ref.shape[0] * n_chunks)
    sls = [(c // n_chunks, pl.ds((c % n_chunks) * c_len, c_len)) for c in chunks]
    each = lambda fn, *cols: [fn(*args) for args in zip(*cols)]
    lw = [lw_ref[sl] for sl in sls]
    v = [v_ref[sl] for sl in sls]
    cum = each(lambda x: _dot_rsplit(tril, x), lw)
    e_pos = each(jnp.exp, cum)
    e_neg = each(lambda x: jnp.exp(-x), cum)
    g_last = [x[c_len - 1:c_len, :] for x in e_pos]
    at = each(lambda sl, x, y: -kk_ref[sl] * jnp.exp(x - y), sls, cum, lw)
    rt = each(lambda sl, x: r_ref[sl] * x, sls, e_pos)
    bt = each(lambda sl, x: b_ref[sl] * x, sls, e_neg)
    kt = each(lambda sl, x: k_ref[sl] * x, sls, e_neg)
    ar = each(lambda x, y: jnp.concatenate([x, y], axis=0).astype(BF16), at, rt)
    mb = each(_dot_nt, ar, bt)
    mk = each(_dot_nt, ar, kt)
    aab = [jnp.where(strict, x[:c_len], 0.0) for x in mb]
    arb = [jnp.where(incl, x[c_len:], 0.0) for x in mb]
    aak = [jnp.where(strict, x[:c_len], 0.0) for x in mk]
    ark = [jnp.where(incl, x[c_len:], 0.0) for x in mk]
    d1 = [jnp.where(m16, x, 0.0) for x in aab]
    d2 = each(_dot, d1, d1)
    t1 = each(lambda x, y: _dot(eye + x, eye + y), d1, d2)
    d4 = each(_dot, d2, d2)
    t2 = each(lambda x, y: _dot(x, eye + y), t1, d4)
    d8 = each(_dot, d4, d4)
    t3 = each(lambda x, y: _dot(x, eye + y), t2, d8)
    x32 = each(lambda t, a: _dot(t, jnp.where(m32 & jnp.logical_not(m16), a, 0.0)), t3, aab)
    t32 = each(lambda t, x: t + _dot(x, t), t3, x32)
    x64 = each(lambda t, a: _dot(t, jnp.where(m32, 0.0, a)), t32, aab)
    tinv = each(lambda t, x: t + _dot(x, t), t32, x64)
    akv = each(_dot, aak, v)
    atp = each(_dot, tinv, at)
    w2 = each(_dot, tinv, akv)
    y2 = each(_dot, ark, v)
    z2 = each(lambda x, y, g: _dot_tn(x, y * g), v, kt, g_last)
    for c in chunks:
        sl = sls[c]
        atp_ref[sl] = atp[c].astype(BF16)
        rt_ref[sl] = rt[c].astype(BF16)
        arb_ref[sl] = arb[c].astype(BF16)
        bh_ref[sl] = (bt[c] * g_last[c]).astype(BF16)
        w2_ref[sl] = w2[c]
        y2_ref[sl] = y2[c]
        z2_ref[sl] = z2[c]
        gc_ref[c // n_chunks, c % n_chunks:c % n_chunks + 1, :] = g_last[c]


def _rwkv_scan_kernel(atp_ref, rt_ref, arb_ref, bh_ref, w2_ref, y2_ref, z2_ref, gc_ref,
                      y_ref, s_ref):
    c_len = CHUNK
    n_chunks = atp_ref.shape[1] // c_len
    group = atp_ref.shape[0]

    @pl.when(pl.program_id(1) == 0)
    def _():
        s_ref[...] = jnp.zeros_like(s_ref)

    def body(c, states):
        sl = pl.ds(pl.multiple_of(c * c_len, c_len), c_len)
        heads = range(group)
        sb = [states[gi].astype(BF16) for gi in heads]
        u = [_dot_nt(atp_ref[gi, sl, :], sb[gi]) + w2_ref[gi, sl, :] for gi in heads]
        ys = [_dot_nt(rt_ref[gi, sl, :], sb[gi]) + y2_ref[gi, sl, :] for gi in heads]
        ub = [x.astype(BF16) for x in u]
        upd = [_dot_tn(ub[gi], bh_ref[gi, sl, :]) for gi in heads]
        for gi in heads:
            y_ref[gi, sl, :] = ys[gi] + _dot(arb_ref[gi, sl, :], ub[gi])
        return tuple(states[gi] * gc_ref[gi, pl.ds(c, 1), :] + upd[gi] + z2_ref[gi, sl, :]
                     for gi in heads)

    states = lax.fori_loop(0, n_chunks, body, tuple(s_ref[gi] for gi in range(group)))
    for gi in range(group):
        s_ref[gi] = states[gi]


def _rwkv_scan(r, lw, k, v, kk, b):
    bh, s, n = r.shape
    sb = min(SCAN_BLOCK, s)
    nc = sb // CHUNK
    hp = CHUNK_PREP_HEADS
    blk = pl.BlockSpec((hp, sb, n), lambda i, j: (i, j, 0))
    gblk = pl.BlockSpec((hp, nc, n), lambda i, j: (i, j, 0))
    outs = pl.pallas_call(
        _rwkv_chunk_prep_kernel,
        grid=(bh // hp, s // sb),
        in_specs=[blk] * 6,
        out_specs=[blk] * 7 + [gblk],
        out_shape=[jax.ShapeDtypeStruct((bh, s, n), BF16)] * 4
        + [jax.ShapeDtypeStruct((bh, s, n), F32)] * 3
        + [jax.ShapeDtypeStruct((bh, s // CHUNK, n), F32)],
        compiler_params=_cparams("parallel", "parallel"),
    )(r, lw, k, v, kk, b)
    grp = SCAN_GROUP
    blk = pl.BlockSpec((grp, sb, n), lambda i, j: (i, j, 0))
    gblk = pl.BlockSpec((grp, nc, n), lambda i, j: (i, j, 0))
    return pl.pallas_call(
        _rwkv_scan_kernel,
        grid=(bh // grp, s // sb),
        in_specs=[blk] * 7 + [gblk],
        out_specs=blk,
        out_shape=jax.ShapeDtypeStruct((bh, s, n), F32),
        scratch_shapes=[pltpu.VMEM((grp, n, n), F32)],
        compiler_params=_cparams("parallel", "arbitrary"),
    )(*outs)


def _moba_prep_kernel(q_ref, k_ref, v_ref, gain_ref, ones_ref, qo_ref, qt_ref, ko_ref, vt_ref, km_ref):
    ones_bd = ones_ref[...]
    q = q_ref[...]
    k = k_ref[...]
    inv_d = 1.0 / HEAD_DIM
    qn = q * lax.rsqrt(_dot_lsplit(q * q, ones_bd) * inv_d + QK_EPS) * gain_ref[0:1, :]
    kn = k * lax.rsqrt(_dot_lsplit(k * k, ones_bd) * inv_d + QK_EPS) * gain_ref[1:2, :]
    qs = qn * (HEAD_DIM ** -0.5 * LOG2E)
    qo_ref[...] = qs
    qt_ref[0] = qs.T.astype(BF16)
    _store_heads(ko_ref, kn)
    vt_ref[0] = v_ref[...].T.astype(BF16)
    km_ref[0] = jnp.mean(kn, axis=0, keepdims=True)


def _moba_prep(proj, seq, gains, ones_bd):
    t = proj.shape[0]
    ts = MOBA_BLOCK
    tps = seq // ts
    col = lambda c: pl.BlockSpec((ts, WIDTH), lambda i, c=c: (i, c))
    tspec = pl.BlockSpec((1, WIDTH, ts), lambda i: (i // tps, 0, i % tps))
    tshape = jax.ShapeDtypeStruct((t // seq, WIDTH, seq), BF16)
    return pl.pallas_call(
        _moba_prep_kernel,
        grid=(t // ts,),
        in_specs=[col(3), col(4), col(5),
                  pl.BlockSpec((2, WIDTH), lambda i: (0, 0)),
                  pl.BlockSpec((WIDTH, WIDTH), lambda i: (0, 0))],
        out_specs=[pl.BlockSpec((ts, WIDTH), lambda i: (i, 0)), tspec,
                   pl.BlockSpec((1, HEADS, ts, HEAD_DIM), lambda i: (i // tps, 0, i % tps, 0)),
                   tspec, pl.BlockSpec((1, 1, WIDTH), lambda i: (i, 0, 0))],
        out_shape=[jax.ShapeDtypeStruct((t, WIDTH), F32), tshape,
                   jax.ShapeDtypeStruct((t // seq, HEADS, seq, HEAD_DIM), BF16),
                   tshape, jax.ShapeDtypeStruct((t // ts, 1, WIDTH), F32)],
        compiler_params=_cparams("parallel"),
    )(proj, proj, proj, gains, ones_bd)


def _gating_kernel(q_ref, km_ref, o_ref):
    ah, al = _split(km_ref[0])
    bh, bl = _split(q_ref[...])
    nt = lambda a, b: lax.dot_general(a, b, (((1,), (1,)), ((), ())), preferred_element_type=F32)
    o_ref[0] = nt(ah, bh) + nt(al, bh) + nt(ah, bl)


def _gating_scores(qf, km_t, batch):
    t = qf.shape[0]
    ts = ROW_TILE
    per = t // batch // ts
    rows = km_t.shape[1]
    return pl.pallas_call(
        _gating_kernel,
        grid=(batch, per),
        in_specs=[pl.BlockSpec((ts, WIDTH), lambda b, i: (b * per + i, 0)),
                  pl.BlockSpec((1, rows, WIDTH), lambda b, i: (b, 0, 0))],
        out_specs=pl.BlockSpec((1, rows, ts), lambda b, i: (b, 0, i)),
        out_shape=jax.ShapeDtypeStruct((batch, rows, t // batch), F32),
        compiler_params=_cparams("parallel", "parallel"),
    )(qf, km_t)


def _rel_bias_kernel(tab_ref, o_ref):
    h = pl.program_id(0)
    blk = MOBA_BLOCK
    key = lax.broadcasted_iota(jnp.int32, (blk, blk), 0)
    qry = lax.broadcasted_iota(jnp.int32, (blk, blk), 1)
    max_exact = REL_BUCKETS // 2

    def lookup(n):
        nf = jnp.maximum(n, 1).astype(F32)
        large = max_exact + (jnp.log(nf / max_exact) / math.log(REL_MAX_DIST / max_exact)
                             * (REL_BUCKETS - max_exact)).astype(jnp.int32)
        bucket = jnp.where(n < max_exact, n, jnp.minimum(large, REL_BUCKETS - 1))
        out = jnp.zeros((blk, blk), F32)
        for bkt in range(REL_BUCKETS):
            out = jnp.where(bucket == bkt, tab_ref[bkt, h], out)
        return out

    dist = qry - key
    far = lookup(dist + 2 * blk)
    o_ref[0, 0] = jnp.where(dist >= 0, (lookup(jnp.maximum(dist, 0)) - far) * LOG2E, NEG_BIG)
    o_ref[0, 1] = (lookup(dist + blk) - far) * LOG2E


def _rel_bias_tiles(rel_bias):
    return pl.pallas_call(
        _rel_bias_kernel,
        grid=(HEADS,),
        in_specs=[pl.BlockSpec(memory_space=pltpu.SMEM)],
        out_specs=pl.BlockSpec((1, 2, MOBA_BLOCK, MOBA_BLOCK), lambda h: (h, 0, 0, 0)),
        out_shape=jax.ShapeDtypeStruct((HEADS, 2, MOBA_BLOCK, MOBA_BLOCK), F32),
        compiler_params=_cparams("parallel"),
    )(rel_bias)


def _moba_attn_kernel(qt_ref, k_ref, vt_ref, sc_ref, bias_ref, o_ref, mask_ref, sa_ref, sb_ref,
                      pend_ref):
    blk = MOBA_BLOCK
    i = pl.program_id(1)
    nb = sc_ref.shape[1]
    heads = range(qt_ref.shape[0])
    dh = vt_ref.shape[1]
    bidx = lax.broadcasted_iota(jnp.int32, (nb, blk), 0)
    past = bidx < i
    ones_rows = jnp.ones((SUM_ROWS, blk), BF16)

    def block(j):
        return pl.ds(pl.multiple_of(j * blk, blk), blk)

    def scores(hh, j):
        return jnp.dot(k_ref[hh, block(j), :], qt_ref[hh], preferred_element_type=F32)

    def weighted_values(hh, j, p):
        vals = jnp.concatenate([vt_ref[hh, :, block(j)], ones_rows], axis=0)
        return jnp.dot(vals, p, preferred_element_type=F32)

    def softmax_step(s, m, mrow):
        m_new = jnp.maximum(m, jnp.max(s, axis=0, keepdims=True) + mrow)
        return m_new, jnp.exp2(m - m_new), jnp.exp2((s - (m_new - mrow)).astype(BF16))

    j_prev = jnp.maximum(i - 1, 0)
    s_own = [scores(hh, i) + bias_ref[hh, 0] for hh in heads]
    s_prev = [scores(hh, j_prev) + bias_ref[hh, 1] for hh in heads]
    n_far = jnp.maximum(i - 1, 0)
    last_far = jnp.maximum(n_far - 1, 0)

    def store_scores(s_ref, j):
        for hh in heads:
            s_ref[hh] = scores(hh, jnp.minimum(j, last_far))

    store_scores(sa_ref, 0)

    for hh in heads:
        scm = jnp.where(past, sc_ref[hh], -jnp.inf)
        sel = jnp.zeros((nb, blk), jnp.bool_)
        for _ in range(MOBA_TOPK):
            top = jnp.max(scm, axis=0, keepdims=True)
            idx = jnp.min(jnp.where(scm == top, bidx, nb), axis=0, keepdims=True)
            pick = bidx == idx
            sel = jnp.logical_or(sel, pick)
            scm = jnp.where(pick, -jnp.inf, scm)
        mask_ref[hh] = jnp.where(jnp.logical_and(sel, past), 0.0, NEG_BIG)

    ms, accs, pend = [], [], []
    for hh in heads:
        m = jnp.max(s_own[hh], axis=0, keepdims=True)
        accs.append(weighted_values(hh, i, jnp.exp2((s_own[hh] - m).astype(BF16))))
        ms.append(m)
    for hh in heads:
        mrow = jnp.where(i > 0, mask_ref[hh, pl.ds(j_prev, 1), :], NEG_BIG)
        ms[hh], alpha, pend_ref[hh] = softmax_step(s_prev[hh], ms[hh], mrow)
        pend.append(alpha)

    def far_body(t, carry):
        j_pend, ms, accs, pend = carry
        j0 = 2 * t
        j1 = jnp.minimum(j0 + 1, last_far)
        pv_pend = [weighted_values(hh, j_pend, pend_ref[hh]) for hh in heads]
        store_scores(sb_ref, j0 + 1)
        steps = [softmax_step(sa_ref[hh], ms[hh], mask_ref[hh, pl.ds(j0, 1), :]) for hh in heads]
        accs = [pend[hh] * accs[hh] + pv_pend[hh] for hh in heads]
        pv0 = [weighted_values(hh, j0, steps[hh][2]) for hh in heads]
        store_scores(sa_ref, j0 + 2)
        new_ms, new_pend = [], []
        for hh in heads:
            mrow = jnp.where(j0 + 1 < n_far, mask_ref[hh, pl.ds(j1, 1), :], NEG_BIG)
            m, alpha, pend_ref[hh] = softmax_step(sb_ref[hh], steps[hh][0], mrow)
            new_ms.append(m)
            new_pend.append(alpha)
        accs = [steps[hh][1] * accs[hh] + pv0[hh] for hh in heads]
        return j1, tuple(new_ms), tuple(accs), tuple(new_pend)

    j_pend, ms, accs, pend = lax.fori_loop(
        0, (n_far + 1) // 2, far_body, (j_prev, tuple(ms), tuple(accs), tuple(pend)))
    for hh in heads:
        acc = pend[hh] * accs[hh] + weighted_values(hh, j_pend, pend_ref[hh])
        o_ref[hh] = acc[:dh] / acc[dh:dh + 1]


def _moba_attn(qt, k, vt, sc_t, bias):
    bh, s, dh = k.shape
    nb = sc_t.shape[1]
    blk = MOBA_BLOCK
    hp = ATTN_HEADS_PER_STEP
    return pl.pallas_call(
        _moba_attn_kernel,
        grid=(bh // hp, s // blk),
        in_specs=[pl.BlockSpec((hp, dh, blk), lambda b, i: (b, 0, i)),
                  pl.BlockSpec((hp, s, dh), lambda b, i: (b, 0, 0)),
                  pl.BlockSpec((hp, dh, s), lambda b, i: (b, 0, 0)),
                  pl.BlockSpec((hp, nb, blk), lambda b, i: (b, 0, i)),
                  pl.BlockSpec((hp, 2, blk, blk), lambda b, i: (b % (HEADS // hp), 0, 0, 0))],
        out_specs=pl.BlockSpec((hp, dh, blk), lambda b, i: (b, 0, i)),
        out_shape=jax.ShapeDtypeStruct((bh, dh, s), F32),
        scratch_shapes=[pltpu.VMEM((hp, nb, blk), F32), pltpu.VMEM((hp, blk, blk), F32),
                        pltpu.VMEM((hp, blk, blk), F32), pltpu.VMEM((hp, blk, blk), BF16)],
        compiler_params=_cparams("parallel", "parallel"),
    )(qt, k, vt, sc_t, bias)


def _merge_router_kernel(x_ref, ys_ref, bonus_ref, g_ref, ym_ref, gr_ref, gm_ref,
                         lnx_ref, ones_ref, wr_ref, wm_ref, wo_ref, gf_ref, wrt_ref, brt_ref,
                         xo_ref, xn_ref, route_ref):
    ones_bd = ones_ref[...]
    inv_n = 1.0 / HEAD_DIM
    y = _load_heads(ys_ref)
    mean = _dot_lsplit(y, ones_bd) * inv_n
    d = y - mean
    var = _dot_lsplit(d * d, ones_bd) * inv_n
    yr = d * lax.rsqrt(var + LN_X_EPS) * lnx_ref[0:1, :] + lnx_ref[1:2, :]
    yr = (yr + bonus_ref[...]) * g_ref[...]
    h = (_sigmoid(gr_ref[...]) * _dot(yr, wr_ref[...])
         + _sigmoid(gm_ref[...]) * _dot(ym_ref[0].T, wm_ref[...]))
    x = x_ref[...] + _dot(h, wo_ref[...])
    xo_ref[...] = x

    ms = jnp.mean(x * x, axis=-1, keepdims=True)
    xn = x * lax.rsqrt(ms + NORM_EPS) * gf_ref[...]
    xn_ref[...] = xn

    logits = _dot_x3(xn, wrt_ref[...]) + brt_ref[...]
    lane = lax.broadcasted_iota(jnp.int32, logits.shape, 1)
    big = logits.shape[1]
    is_grp = lane < N_GROUPS
    lg = jnp.where(is_grp, logits, -jnp.inf)
    mg = jnp.max(lg, axis=1, keepdims=True)
    g_sel = jnp.min(jnp.where(lg == mg, lane, big), axis=1, keepdims=True)
    p_top = 1.0 / jnp.sum(jnp.where(is_grp, jnp.exp(logits - mg), 0.0), axis=1, keepdims=True)
    lo_lane = N_GROUPS + g_sel * EXPERTS_PER_GROUP
    in_grp = jnp.logical_and(lane >= lo_lane, lane < lo_lane + EXPERTS_PER_GROUP)
    le = jnp.where(in_grp, logits, -jnp.inf)
    m1 = jnp.max(le, axis=1, keepdims=True)
    i1 = jnp.min(jnp.where(le == m1, lane, big), axis=1, keepdims=True)
    le2 = jnp.where(lane == i1, -jnp.inf, le)
    m2 = jnp.max(le2, axis=1, keepdims=True)
    i2 = jnp.min(jnp.where(le2 == m2, lane, big), axis=1, keepdims=True)
    e2 = jnp.exp(m2 - m1)
    gate1 = p_top / (1.0 + e2)
    gate2 = p_top * e2 / (1.0 + e2)
    route = jnp.where(lane == 0, (i1 - N_GROUPS).astype(F32),
                      jnp.where(lane == 1, (i2 - N_GROUPS).astype(F32),
                                jnp.where(lane == 2, gate1, jnp.where(lane == 3, gate2, 0.0))))
    route_ref[...] = route


def _merge_router(x, ys, bonus, g, ym_t, proj, lnx, ones_bd, wr, wm, wo, gf, wrt, brt):
    t, d = x.shape
    tm = MERGE_TILE
    tps = ys.shape[2] // tm
    row = lambda w: pl.BlockSpec((tm, w), lambda i: (i, 0))
    full = lambda shape: pl.BlockSpec(shape, lambda i: tuple(0 for _ in shape))
    return pl.pallas_call(
        _merge_router_kernel,
        grid=(t // tm,),
        in_specs=[row(d),
                  pl.BlockSpec((1, HEADS, tm, HEAD_DIM), lambda i: (i // tps, 0, i % tps, 0)),
                  row(WIDTH), row(WIDTH),
                  pl.BlockSpec((1, WIDTH, tm), lambda i: (i // tps, 0, i % tps)),
                  pl.BlockSpec((tm, d), lambda i: (i, 3)),
                  pl.BlockSpec((tm, d), lambda i: (i, 4)),
                  full((2, WIDTH)), full((WIDTH, WIDTH)), full((WIDTH, d)), full((WIDTH, d)),
                  full((d, d)), full((1, d)), full((d, 128)), full((1, 128))],
        out_specs=[row(d), row(d), row(128)],
        out_shape=[jax.ShapeDtypeStruct((t, d), F32), jax.ShapeDtypeStruct((t, d), F32),
                   jax.ShapeDtypeStruct((t, 128), F32)],
        compiler_params=_cparams("parallel"),
    )(x, ys, bonus, g, ym_t, proj, proj, lnx, ones_bd, wr, wm, wo, gf, wrt, brt)


def _moe_dispatch_kernel(pos_ref, x_ref, xs_in, xs_out, sem):
    del xs_in
    i = pl.program_id(0)
    tm = x_ref.shape[0]

    def row_copy(r, j):
        p = pos_ref[(i * tm + r) * 2 + j]
        return pltpu.make_async_copy(x_ref.at[pl.ds(r, 1), :], xs_out.at[pl.ds(p, 1), :], sem)

    def start(r, c):
        row_copy(r, 0).start()
        row_copy(r, 1).start()
        return c

    def wait(r, c):
        row_copy(r, 0).wait()
        row_copy(r, 1).wait()
        return c

    lax.fori_loop(0, tm, start, 0, unroll=DMA_UNROLL)
    lax.fori_loop(0, tm, wait, 0, unroll=DMA_UNROLL)


def _moe_dispatch(pos, xn, n_rows):
    t, d = xn.shape
    tm = MOE_ROW_TILE
    grid_spec = pltpu.PrefetchScalarGridSpec(
        num_scalar_prefetch=1,
        grid=(t // tm,),
        in_specs=[pl.BlockSpec((tm, d), lambda i, p: (i, 0)),
                  pl.BlockSpec(memory_space=pl.ANY)],
        out_specs=pl.BlockSpec(memory_space=pl.ANY),
        scratch_shapes=[pltpu.SemaphoreType.DMA(())],
    )
    return pl.pallas_call(
        _moe_dispatch_kernel,
        grid_spec=grid_spec,
        out_shape=jax.ShapeDtypeStruct((n_rows, d), F32),
        input_output_aliases={2: 0},
        compiler_params=_cparams("arbitrary"),
    )(pos, xn, jnp.zeros((n_rows, d), F32))


def _moe_ffn_kernel(texp_ref, nused_ref, x_ref, wg_ref, wu_ref, wd_ref, o_ref, wg_bf, wu_bf, wd_bf):
    i = pl.program_id(0)
    used = i < nused_ref[0]
    new_expert = jnp.logical_or(i == 0, texp_ref[i] != texp_ref[jnp.maximum(i - 1, 0)])

    @pl.when(jnp.logical_and(used, new_expert))
    def _():
        wg_bf[...] = wg_ref[0, 0].astype(BF16)
        wu_bf[...] = wu_ref[0, 0].astype(BF16)
        wd_bf[...] = wd_ref[0, 0].astype(BF16)

    @pl.when(used)
    def _():
        xb = x_ref[...].astype(BF16)
        hg = jnp.dot(xb, wg_bf[...], preferred_element_type=F32)
        hu = jnp.dot(xb, wu_bf[...], preferred_element_type=F32)
        h = hg * _sigmoid(hg) * hu
        o_ref[...] = jnp.dot(h.astype(BF16), wd_bf[...], preferred_element_type=F32)

    @pl.when(jnp.logical_not(used))
    def _():
        o_ref[...] = jnp.zeros_like(o_ref)


def _moe_ffn(tile_expert, nused, xs, wg, wu, wd, layer):
    n_rows, d = xs.shape
    tm = MOE_TILE
    dh = wg.shape[3]
    grid_spec = pltpu.PrefetchScalarGridSpec(
        num_scalar_prefetch=2,
        grid=(n_rows // tm,),
        in_specs=[pl.BlockSpec((tm, d), lambda i, te, nu: (jnp.minimum(i, nu[0] - 1), 0)),
                  pl.BlockSpec((1, 1, d, dh), lambda i, te, nu: (layer, te[i], 0, 0)),
                  pl.BlockSpec((1, 1, d, dh), lambda i, te, nu: (layer, te[i], 0, 0)),
                  pl.BlockSpec((1, 1, dh, d), lambda i, te, nu: (layer, te[i], 0, 0))],
        out_specs=pl.BlockSpec((tm, d), lambda i, te, nu: (i, 0)),
        scratch_shapes=[pltpu.VMEM((d, dh), BF16), pltpu.VMEM((d, dh), BF16),
                        pltpu.VMEM((dh, d), BF16)],
    )
    return pl.pallas_call(
        _moe_ffn_kernel,
        grid_spec=grid_spec,
        out_shape=jax.ShapeDtypeStruct((n_rows, d), F32),
        compiler_params=_cparams("arbitrary"),
    )(tile_expert, nused, xs, wg, wu, wd)


def _moe_combine_kernel(pos_ref, ys_hbm, x_ref, route_ref, o_ref, buf0, buf1, sem):
    i = pl.program_id(0)
    tm = buf0.shape[0]

    def row_copy(r, j, buf):
        p = pos_ref[(i * tm + r) * 2 + j]
        return pltpu.make_async_copy(ys_hbm.at[pl.ds(p, 1), :], buf.at[pl.ds(r, 1), :], sem)

    def start(r, c):
        row_copy(r, 0, buf0).start()
        row_copy(r, 1, buf1).start()
        return c

    def wait(r, c):
        row_copy(r, 0, buf0).wait()
        row_copy(r, 1, buf1).wait()
        return c

    lax.fori_loop(0, tm, start, 0, unroll=DMA_UNROLL)
    lax.fori_loop(0, tm, wait, 0, unroll=DMA_UNROLL)
    route = route_ref[...]
    o_ref[...] = x_ref[...] + route[:, 2:3] * buf0[...] + route[:, 3:4] * buf1[...]


def _moe_combine(pos, ys, x, route):
    t, d = x.shape
    tm = MOE_ROW_TILE
    grid_spec = pltpu.PrefetchScalarGridSpec(
        num_scalar_prefetch=1,
        grid=(t // tm,),
        in_specs=[pl.BlockSpec(memory_space=pl.ANY),
                  pl.BlockSpec((tm, d), lambda i, p: (i, 0)),
                  pl.BlockSpec((tm, 128), lambda i, p: (i, 0))],
        out_specs=pl.BlockSpec((tm, d), lambda i, p: (i, 0)),
        scratch_shapes=[pltpu.VMEM((tm, d), F32), pltpu.VMEM((tm, d), F32),
                        pltpu.SemaphoreType.DMA(())],
    )
    return pl.pallas_call(
        _moe_combine_kernel,
        grid_spec=grid_spec,
        out_shape=jax.ShapeDtypeStruct((t, d), F32),
        compiler_params=_cparams("arbitrary"),
    )(pos, ys, x, route)


def _moe_plan(route):
    tm = MOE_TILE
    e_flat = route[:, 0:2].astype(jnp.int32).reshape(-1)
    n_rows = e_flat.shape[0] + N_EXPERTS * tm
    onehot = (e_flat[:, None] == jnp.arange(N_EXPERTS, dtype=jnp.int32)[None, :]).astype(jnp.int32)
    running = jnp.cumsum(onehot, axis=0)
    counts = running[-1]
    padded = ((counts + tm - 1) // tm) * tm
    pend = jnp.cumsum(padded)
    pstart = pend - padded
    pos = jnp.sum(onehot * (running - 1 + pstart[None, :]), axis=1).astype(jnp.int32)
    tile_start = jnp.arange(n_rows // tm, dtype=jnp.int32) * tm
    tile_expert = jnp.minimum(jnp.sum(pend[None, :] <= tile_start[:, None], axis=1),
                              N_EXPERTS - 1).astype(jnp.int32)
    nused = (pend[-1:] // tm).astype(jnp.int32)
    return tile_expert, nused, pos, n_rows


def _pad_cols(w, width):
    return jnp.pad(w, ((0, 0), (0, width - w.shape[1])))


def _pad_rows(w, rows):
    return jnp.pad(w, ((0, rows - w.shape[0]), (0, 0)))


def kernel(x, norm_mix_g, norm_ffn_g, w_in, mu_rkv, mu_x, w0, w1, w2, a0, a1, a2, g1, g2, k_k, k_a, r_k, lnx_g, lnx_b, vres_mu, vres_v0, vres_v1, vres_v2, q_gain, k_gain, rel_bias, w_br_rwkv, w_br_moba, w_out, w_grp, b_grp, w_exp, b_exp, w_gate, w_up, w_down):
    batch, seq, d_model = x.shape
    depth = w_in.shape[0]
    t = batch * seq
    nb = seq // MOBA_BLOCK
    assert seq % MOBA_BLOCK == 0 and seq % SCAN_BLOCK == 0 and (batch * HEADS) % SCAN_GROUP == 0

    head_of = jnp.arange(WIDTH) // HEAD_DIM
    ones_bd = (head_of[:, None] == head_of[None, :]).astype(BF16)
    bias_tiles = _rel_bias_tiles(rel_bias)
    gains = jnp.stack([jnp.tile(q_gain, (1, HEADS)), jnp.tile(k_gain, (1, HEADS))], axis=1)

    xf = x.reshape(t, d_model)
    v_first = None
    for l in range(depth):
        has_vres = l > 0
        lora_w = [w1[l], a1[l], g1[l]]
        lora_mu = [mu_x[l, 0], mu_x[l, 1], mu_x[l, 2]]
        if has_vres:
            lora_w.append(vres_v1[l - 1])
            lora_mu.append(vres_mu[l - 1])
        p1 = _pad_cols(jnp.concatenate([w * (1.0 - m)[:, None] for w, m in zip(lora_w, lora_mu)], 1),
                       LORA_PAD)
        p2 = _pad_cols(jnp.concatenate([w * m[:, None] for w, m in zip(lora_w, lora_mu)], 1), LORA_PAD)
        w_all = jnp.concatenate([w_in[l], p1, p2], axis=1).astype(BF16)
        wwa = jnp.concatenate([
            jnp.concatenate([w2[l], jnp.zeros((DECAY_LORA, WIDTH), F32)], 1),
            jnp.concatenate([jnp.zeros((AAA_LORA, WIDTH), F32), a2[l]], 1)], 0).astype(BF16)
        v2p = (_pad_rows(vres_v2[l - 1], 128) if has_vres else jnp.zeros((128, WIDTH), F32)).astype(BF16)
        v0 = vres_v0[l - 1] if has_vres else jnp.zeros((WIDTH,), F32)
        vecs = jnp.stack([w0[l], a0[l], k_k[l], k_a[l], r_k[l].reshape(-1), v0,
                          jnp.zeros((WIDTH,), F32), jnp.zeros((WIDTH,), F32)])

        proj = _norm_proj(xf, norm_mix_g[l], w_all)

        *heads, v_row, g, bonus = _rwkv_prep(proj, seq, mu_rkv[l], vecs, wwa,
                                             g2[l].astype(BF16), v2p, ones_bd, v_first)
        if l == 0:
            v_first = v_row
        bh = batch * HEADS
        y_scan = _rwkv_scan(*[a_.reshape(bh, seq, HEAD_DIM) for a_ in heads]).reshape(
            batch, HEADS, seq, HEAD_DIM)

        qf, q_t, k_h, v_t, kmean = _moba_prep(proj, seq, gains[l], ones_bd)
        km = kmean.reshape(batch, nb, HEADS, HEAD_DIM)
        km_t = jnp.einsum("bjhd,hg->bgjhd", km, jnp.eye(HEADS, dtype=F32)).reshape(
            batch, HEADS * nb, WIDTH)
        sc_t = _gating_scores(qf, km_t, batch).reshape(bh, nb, seq)
        y_m = _moba_attn(q_t.reshape(bh, HEAD_DIM, seq), k_h.reshape(bh, seq, HEAD_DIM),
                         v_t.reshape(bh, HEAD_DIM, seq), sc_t, bias_tiles).reshape(
                             batch, WIDTH, seq)

        wrt = _pad_cols(jnp.concatenate([w_grp[l], w_exp[l]], axis=1), 128)
        brt = _pad_cols(jnp.concatenate([b_grp[l], b_exp[l]])[None, :], 128)
        lnx = jnp.stack([lnx_g[l], lnx_b[l]])
        xf, xn2, route = _merge_router(
            xf, y_scan, bonus, g, y_m, proj, lnx, ones_bd,
            w_br_rwkv[l].astype(BF16), w_br_moba[l].astype(BF16), w_out[l].astype(BF16),
            norm_ffn_g[l].reshape(1, d_model), wrt, brt)

        tile_expert, nused, pos, n_rows = _moe_plan(route)
        xs = _moe_dispatch(pos, xn2, n_rows)
        ys = _moe_ffn(tile_expert, nused, xs, w_gate, w_up, w_down, l)
        xf = _moe_combine(pos, ys, xf, route)
    return xf.reshape(batch, seq, d_model)
```

```python
import functools
import math

import jax
import jax.numpy as jnp
from jax import lax
from jax.experimental import pallas as pl
from jax.experimental.pallas import tpu as pltpu

F32 = jnp.float32
BF16 = jnp.bfloat16

HEADS = 8
HEAD_DIM = 64
WIDTH = HEADS * HEAD_DIM
DECAY_LORA, AAA_LORA, MV_LORA, GATE_LORA = 64, 64, 32, 128
LN_X_EPS = 64e-5
MOBA_BLOCK = 256
MOBA_TOPK = 3
REL_BUCKETS = 32
REL_MAX_DIST = 128
N_GROUPS = 4
EXPERTS_PER_GROUP = 8
N_EXPERTS = N_GROUPS * EXPERTS_PER_GROUP
NORM_EPS = 1e-6
QK_EPS = 1e-6

CHUNK = 64
SCAN_BLOCK = 512
SCAN_GROUP = 8
CHUNK_PREP_HEADS = 4
LORA_PAD = 512
ROW_TILE = 256
MOE_TILE = 512
MOE_ROW_TILE = 512
MERGE_TILE = 512
DMA_UNROLL = 8
NEG_BIG = -1e30
LOG2E = math.log2(math.e)
ATTN_HEADS_PER_STEP = 4
SUM_ROWS = 16
VMEM_LIMIT = 48 * 1024 * 1024


def _cparams(*sem):
    return pltpu.CompilerParams(dimension_semantics=sem, vmem_limit_bytes=VMEM_LIMIT)


def _dot(a, b):
    return jnp.dot(a.astype(BF16), b.astype(BF16), preferred_element_type=F32)


def _dot_nt(a, b):
    return lax.dot_general(a.astype(BF16), b.astype(BF16), (((1,), (1,)), ((), ())),
                           preferred_element_type=F32)


def _dot_tn(a, b):
    return lax.dot_general(a.astype(BF16), b.astype(BF16), (((0,), (0,)), ((), ())),
                           preferred_element_type=F32)


def _split(x):
    hi = x.astype(BF16)
    lo = (x - hi.astype(F32)).astype(BF16)
    return hi, lo


def _dot_lsplit(a, b_exact):
    hi, lo = _split(a)
    return (jnp.dot(hi, b_exact, preferred_element_type=F32)
            + jnp.dot(lo, b_exact, preferred_element_type=F32))


def _dot_rsplit(a_exact, b):
    hi, lo = _split(b)
    return (jnp.dot(a_exact, hi, preferred_element_type=F32)
            + jnp.dot(a_exact, lo, preferred_element_type=F32))


def _dot_x3(a, b):
    ah, al = _split(a)
    bh, bl = _split(b)
    return (jnp.dot(ah, bh, preferred_element_type=F32)
            + jnp.dot(al, bh, preferred_element_type=F32)
            + jnp.dot(ah, bl, preferred_element_type=F32))


def _sigmoid(x):
    return 1.0 / (1.0 + jnp.exp(-x))


def _norm_proj_kernel(x_ref, g_ref, w_ref, o_ref, xn_ref):
    @pl.when(pl.program_id(1) == 0)
    def _():
        x = x_ref[...]
        ms = jnp.mean(x * x, axis=-1, keepdims=True)
        xn_ref[...] = (x * lax.rsqrt(ms + NORM_EPS) * g_ref[...]).astype(BF16)

    o_ref[...] = jnp.dot(xn_ref[...], w_ref[...], preferred_element_type=F32)


def _norm_proj(x, g, w, tm=1024, tn=2048):
    m, k = x.shape
    n = w.shape[1]
    return pl.pallas_call(
        _norm_proj_kernel,
        grid=(m // tm, n // tn),
        in_specs=[pl.BlockSpec((tm, k), lambda i, j: (i, 0)),
                  pl.BlockSpec((1, k), lambda i, j: (0, 0)),
                  pl.BlockSpec((k, tn), lambda i, j: (0, j))],
        out_specs=pl.BlockSpec((tm, tn), lambda i, j: (i, j)),
        out_shape=jax.ShapeDtypeStruct((m, n), F32),
        scratch_shapes=[pltpu.VMEM((tm, k), BF16)],
        compiler_params=_cparams("parallel", "arbitrary"),
    )(x, g.reshape(1, k), w)


def _store_heads(o_ref, val):
    for h in range(HEADS):
        o_ref[0, h] = val[:, h * HEAD_DIM:(h + 1) * HEAD_DIM].astype(o_ref.dtype)


def _load_heads(ref):
    return jnp.concatenate([ref[0, h] for h in range(HEADS)], axis=1)


def _shift_rows(x, halo_ref, first):
    prev = jnp.where(first, 0.0, halo_ref[7:8, :])
    rolled = pltpu.roll(x, 1, axis=0)
    row = lax.broadcasted_iota(jnp.int32, x.shape, 0)
    return jnp.where(row == 0, prev, rolled)


def _rwkv_prep_kernel(has_vres, tiles_per_seq, *refs):
    (rp_ref, kp_ref, vp_ref, p1_ref, p2_ref, rph_ref, kph_ref, vph_ref, p2h_ref,
     mu_ref, vec_ref, wwa_ref, g2_ref, v2_ref, ones_ref) = refs[:15]
    refs = refs[15:]
    if has_vres:
        vfirst_ref = refs[0]
        refs = refs[1:]
    r_o, lw_o, k_o, v_o, kk_o, b_o, vrow_o, g_o, bonus_o = refs

    first = (pl.program_id(0) % tiles_per_seq) == 0
    rp, kp, vp = rp_ref[...], kp_ref[...], vp_ref[...]
    r = rp + (_shift_rows(rp, rph_ref, first) - rp) * mu_ref[0:1, :]
    k = kp + (_shift_rows(kp, kph_ref, first) - kp) * mu_ref[1:2, :]
    v = vp + (_shift_rows(vp, vph_ref, first) - vp) * mu_ref[2:3, :]
    lo = p1_ref[...] + _shift_rows(p2_ref[...], p2h_ref, first)

    w0, a0, k_k, k_a = vec_ref[0:1, :], vec_ref[1:2, :], vec_ref[2:3, :], vec_ref[3:4, :]
    r_k, v0 = vec_ref[4:5, :], vec_ref[5:6, :]

    h_wa = lo[:, 0:128]
    lane = lax.broadcasted_iota(jnp.int32, h_wa.shape, 1)
    h_wa = jnp.where(lane < DECAY_LORA, jnp.tanh(h_wa), h_wa)
    wa = _dot(h_wa, wwa_ref[...])
    z = -(w0 + wa[:, :WIDTH])
    softplus = jnp.maximum(z, 0.0) + jnp.log(1.0 + jnp.exp(-jnp.abs(z)))
    logw = -jnp.exp(-softplus - 0.5)
    a = _sigmoid(a0 + wa[:, WIDTH:])
    g = _dot(_sigmoid(lo[:, 128:256]), g2_ref[...])
    if has_vres:
        gate_v = _sigmoid(v0 + _dot(lo[:, 256:384], v2_ref[...]))
        v = v + (vfirst_ref[...] - v) * gate_v

    ones_bd = ones_ref[...]
    kk = k * k_k
    ss = _dot_lsplit(kk * kk, ones_bd)
    kk = kk / jnp.maximum(jnp.sqrt(ss), 1e-12)
    k = k * (1.0 + (a - 1.0) * k_a)
    bonus = _dot_lsplit(r * k * r_k, ones_bd) * v

    _store_heads(r_o, r)
    _store_heads(lw_o, logw)
    _store_heads(k_o, k)
    _store_heads(v_o, v)
    _store_heads(kk_o, kk)
    _store_heads(b_o, kk * a)
    vrow_o[...] = v
    g_o[...] = g
    bonus_o[...] = bonus


def _rwkv_prep(proj, seq, mu_rkv, vecs, wwa, g2, v2p, ones_bd, v_first):
    t = proj.shape[0]
    ts = ROW_TILE
    tps = seq // ts
    has_vres = v_first is not None
    col = lambda c: pl.BlockSpec((ts, WIDTH), lambda i, c=c: (i, c))
    halo = lambda c: pl.BlockSpec((8, WIDTH), lambda i, c=c: (jnp.maximum(i * (ts // 8) - 1, 0), c))
    full = lambda shape: pl.BlockSpec(shape, lambda i: tuple(0 for _ in shape))
    p1c = 5120 // WIDTH
    in_specs = [col(0), col(1), col(2), col(p1c), col(p1c + 1),
                halo(0), halo(1), halo(2), halo(p1c + 1),
                full((3, WIDTH)), full((8, WIDTH)), full((128, 2 * WIDTH)),
                full((128, WIDTH)), full((128, WIDTH)), full((WIDTH, WIDTH))]
    args = [proj] * 9 + [mu_rkv, vecs, wwa, g2, v2p, ones_bd]
    if has_vres:
        in_specs.append(pl.BlockSpec((ts, WIDTH), lambda i: (i, 0)))
        args.append(v_first)
    head_spec = pl.BlockSpec((1, HEADS, ts, HEAD_DIM), lambda i: (i // tps, 0, i % tps, 0))
    head_shape = jax.ShapeDtypeStruct((t // seq, HEADS, seq, HEAD_DIM), F32)
    return pl.pallas_call(
        functools.partial(_rwkv_prep_kernel, has_vres, tps),
        grid=(t // ts,),
        in_specs=in_specs,
        out_specs=[head_spec] * 6 + [pl.BlockSpec((ts, WIDTH), lambda i: (i, 0))] * 3,
        out_shape=[head_shape] * 6 + [jax.ShapeDtypeStruct((t, WIDTH), F32)] * 3,
        compiler_params=_cparams("parallel"),
    )(*args)


def _rwkv_chunk_prep_kernel(r_ref, lw_ref, k_ref, v_ref, kk_ref, b_ref,
                            atp_ref, rt_ref, arb_ref, bh_ref, w2_ref, y2_ref, z2_ref, gc_ref):
    c_len = CHUNK
    n_chunks = r_ref.shape[1] // c_len
    row = lax.broadcasted_iota(jnp.int32, (c_len, c_len), 0)
    col = lax.broadcasted_iota(jnp.int32, (c_len, c_len), 1)
    strict = row > col
    incl = row >= col
    tril = incl.astype(BF16)
    m16 = (row // 16) == (col // 16)
    m32 = (row // 32) == (col // 32)
    eye = (row == col).astype(F32)

    chunks = range(r_ref.shape[0] * n_chunks)
    sls = [(c // n_chunks, pl.ds((c % n_chunks) * c_len, c_len)) for c in chunks]
    each = lambda fn, *cols: [fn(*args) for args in zip(*cols)]
    lw = [lw_ref[sl] for sl in sls]
    v = [v_ref[sl] for sl in sls]
    cum = each(lambda x: _dot_rsplit(tril, x), lw)
    e_pos = each(jnp.exp, cum)
    e_neg = each(lambda x: jnp.exp(-x), cum)
    g_last = [x[c_len - 1:c_len, :] for x in e_pos]
    at = each(lambda sl, x, y: -kk_ref[sl] * jnp.exp(x - y), sls, cum, lw)
    rt = each(lambda sl, x: r_ref[sl] * x, sls, e_pos)
    bt = each(lambda sl, x: b_ref[sl] * x, sls, e_neg)
    kt = each(lambda sl, x: k_ref[sl] * x, sls, e_neg)
    ar = each(lambda x, y: jnp.concatenate([x, y], axis=0).astype(BF16), at, rt)
    mb = each(_dot_nt, ar, bt)
    mk = each(_dot_nt, ar, kt)
    aab = [jnp.where(strict, x[:c_len], 0.0) for x in mb]
    arb = [jnp.where(incl, x[c_len:], 0.0) for x in mb]
    aak = [jnp.where(strict, x[:c_len], 0.0) for x in mk]
    ark = [jnp.where(incl, x[c_len:], 0.0) for x in mk]
    d1 = [jnp.where(m16, x, 0.0) for x in aab]
    d2 = each(_dot, d1, d1)
    t1 = each(lambda x, y: _dot(eye + x, eye + y), d1, d2)
    d4 = each(_dot, d2, d2)
    t2 = each(lambda x, y: _dot(x, eye + y), t1, d4)
    d8 = each(_dot, d4, d4)
    t3 = each(lambda x, y: _dot(x, eye + y), t2, d8)
    x32 = each(lambda t, a: _dot(t, jnp.where(m32 & jnp.logical_not(m16), a, 0.0)), t3, aab)
    t32 = each(lambda t, x: t + _dot(x, t), t3, x32)
    x64 = each(lambda t, a: _dot(t, jnp.where(m32, 0.0, a)), t32, aab)
    tinv = each(lambda t, x: t + _dot(x, t), t32, x64)
    akv = each(_dot, aak, v)
    atp = each(_dot, tinv, at)
    w2 = each(_dot, tinv, akv)
    y2 = each(_dot, ark, v)
    z2 = each(lambda x, y, g: _dot_tn(x, y * g), v, kt, g_last)
    for c in chunks:
        sl = sls[c]
        atp_ref[sl] = atp[c].astype(BF16)
        rt_ref[sl] = rt[c].astype(BF16)
        arb_ref[sl] = arb[c].astype(BF16)
        bh_ref[sl] = (bt[c] * g_last[c]).astype(BF16)
        w2_ref[sl] = w2[c]
        y2_ref[sl] = y2[c]
        z2_ref[sl] = z2[c]
        gc_ref[c // n_chunks, c % n_chunks:c % n_chunks + 1, :] = g_last[c]


def _rwkv_scan_kernel(atp_ref, rt_ref, arb_ref, bh_ref, w2_ref, y2_ref, z2_ref, gc_ref,
                      y_ref, s_ref):
    c_len = CHUNK
    n_chunks = atp_ref.shape[1] // c_len
    group = atp_ref.shape[0]

    @pl.when(pl.program_id(1) == 0)
    def _():
        s_ref[...] = jnp.zeros_like(s_ref)

    def body(c, states):
        sl = pl.ds(pl.multiple_of(c * c_len, c_len), c_len)
        heads = range(group)
        sb = [states[gi].astype(BF16) for gi in heads]
        u = [_dot_nt(atp_ref[gi, sl, :], sb[gi]) + w2_ref[gi, sl, :] for gi in heads]
        ys = [_dot_nt(rt_ref[gi, sl, :], sb[gi]) + y2_ref[gi, sl, :] for gi in heads]
        ub = [x.astype(BF16) for x in u]
        upd = [_dot_tn(ub[gi], bh_ref[gi, sl, :]) for gi in heads]
        for gi in heads:
            y_ref[gi, sl, :] = ys[gi] + _dot(arb_ref[gi, sl, :], ub[gi])
        return tuple(states[gi] * gc_ref[gi, pl.ds(c, 1), :] + upd[gi] + z2_ref[gi, sl, :]
                     for gi in heads)

    states = lax.fori_loop(0, n_chunks, body, tuple(s_ref[gi] for gi in range(group)))
    for gi in range(group):
        s_ref[gi] = states[gi]


def _rwkv_scan(r, lw, k, v, kk, b):
    bh, s, n = r.shape
    sb = min(SCAN_BLOCK, s)
    nc = sb // CHUNK
    hp = CHUNK_PREP_HEADS
    blk = pl.BlockSpec((hp, sb, n), lambda i, j: (i, j, 0))
    gblk = pl.BlockSpec((hp, nc, n), lambda i, j: (i, j, 0))
    outs = pl.pallas_call(
        _rwkv_chunk_prep_kernel,
        grid=(bh // hp, s // sb),
        in_specs=[blk] * 6,
        out_specs=[blk] * 7 + [gblk],
        out_shape=[jax.ShapeDtypeStruct((bh, s, n), BF16)] * 4
        + [jax.ShapeDtypeStruct((bh, s, n), F32)] * 3
        + [jax.ShapeDtypeStruct((bh, s // CHUNK, n), F32)],
        compiler_params=_cparams("parallel", "parallel"),
    )(r, lw, k, v, kk, b)
    grp = SCAN_GROUP
    blk = pl.BlockSpec((grp, sb, n), lambda i, j: (i, j, 0))
    gblk = pl.BlockSpec((grp, nc, n), lambda i, j: (i, j, 0))
    return pl.pallas_call(
        _rwkv_scan_kernel,
        grid=(bh // grp, s // sb),
        in_specs=[blk] * 7 + [gblk],
        out_specs=blk,
        out_shape=jax.ShapeDtypeStruct((bh, s, n), F32),
        scratch_shapes=[pltpu.VMEM((grp, n, n), F32)],
        compiler_params=_cparams("parallel", "arbitrary"),
    )(*outs)


def _moba_prep_kernel(q_ref, k_ref, v_ref, gain_ref, ones_ref, qo_ref, qt_ref, ko_ref, vt_ref, km_ref):
    ones_bd = ones_ref[...]
    q = q_ref[...]
    k = k_ref[...]
    inv_d = 1.0 / HEAD_DIM
    qn = q * lax.rsqrt(_dot_lsplit(q * q, ones_bd) * inv_d + QK_EPS) * gain_ref[0:1, :]
    kn = k * lax.rsqrt(_dot_lsplit(k * k, ones_bd) * inv_d + QK_EPS) * gain_ref[1:2, :]
    qs = qn * (HEAD_DIM ** -0.5 * LOG2E)
    qo_ref[...] = qs
    qt_ref[0] = qs.T.astype(BF16)
    _store_heads(ko_ref, kn)
    vt_ref[0] = v_ref[...].T.astype(BF16)
    km_ref[0] = jnp.mean(kn, axis=0, keepdims=True)


def _moba_prep(proj, seq, gains, ones_bd):
    t = proj.shape[0]
    ts = MOBA_BLOCK
    tps = seq // ts
    col = lambda c: pl.BlockSpec((ts, WIDTH), lambda i, c=c: (i, c))
    tspec = pl.BlockSpec((1, WIDTH, ts), lambda i: (i // tps, 0, i % tps))
    tshape = jax.ShapeDtypeStruct((t // seq, WIDTH, seq), BF16)
    return pl.pallas_call(
        _moba_prep_kernel,
        grid=(t // ts,),
        in_specs=[col(3), col(4), col(5),
                  pl.BlockSpec((2, WIDTH), lambda i: (0, 0)),
                  pl.BlockSpec((WIDTH, WIDTH), lambda i: (0, 0))],
        out_specs=[pl.BlockSpec((ts, WIDTH), lambda i: (i, 0)), tspec,
                   pl.BlockSpec((1, HEADS, ts, HEAD_DIM), lambda i: (i // tps, 0, i % tps, 0)),
                   tspec, pl.BlockSpec((1, 1, WIDTH), lambda i: (i, 0, 0))],
        out_shape=[jax.ShapeDtypeStruct((t, WIDTH), F32), tshape,
                   jax.ShapeDtypeStruct((t // seq, HEADS, seq, HEAD_DIM), BF16),
                   tshape, jax.ShapeDtypeStruct((t // ts, 1, WIDTH), F32)],
        compiler_params=_cparams("parallel"),
    )(proj, proj, proj, gains, ones_bd)


def _gating_kernel(q_ref, km_ref, o_ref):
    ah, al = _split(km_ref[0])
    bh, bl = _split(q_ref[...])
    nt = lambda a, b: lax.dot_general(a, b, (((1,), (1,)), ((), ())), preferred_element_type=F32)
    o_ref[0] = nt(ah, bh) + nt(al, bh) + nt(ah, bl)


def _gating_scores(qf, km_t, batch):
    t = qf.shape[0]
    ts = ROW_TILE
    per = t // batch // ts
    rows = km_t.shape[1]
    return pl.pallas_call(
        _gating_kernel,
        grid=(batch, per),
        in_specs=[pl.BlockSpec((ts, WIDTH), lambda b, i: (b * per + i, 0)),
                  pl.BlockSpec((1, rows, WIDTH), lambda b, i: (b, 0, 0))],
        out_specs=pl.BlockSpec((1, rows, ts), lambda b, i: (b, 0, i)),
        out_shape=jax.ShapeDtypeStruct((batch, rows, t // batch), F32),
        compiler_params=_cparams("parallel", "parallel"),
    )(qf, km_t)


def _rel_bias_kernel(tab_ref, o_ref):
    h = pl.program_id(0)
    blk = MOBA_BLOCK
    key = lax.broadcasted_iota(jnp.int32, (blk, blk), 0)
    qry = lax.broadcasted_iota(jnp.int32, (blk, blk), 1)
    max_exact = REL_BUCKETS // 2

    def lookup(n):
        nf = jnp.maximum(n, 1).astype(F32)
        large = max_exact + (jnp.log(nf / max_exact) / math.log(REL_MAX_DIST / max_exact)
                             * (REL_BUCKETS - max_exact)).astype(jnp.int32)
        bucket = jnp.where(n < max_exact, n, jnp.minimum(large, REL_BUCKETS - 1))
        out = jnp.zeros((blk, blk), F32)
        for bkt in range(REL_BUCKETS):
            out = jnp.where(bucket == bkt, tab_ref[bkt, h], out)
        return out

    dist = qry - key
    far = lookup(dist + 2 * blk)
    o_ref[0, 0] = jnp.where(dist >= 0, (lookup(jnp.maximum(dist, 0)) - far) * LOG2E, NEG_BIG)
    o_ref[0, 1] = (lookup(dist + blk) - far) * LOG2E


def _rel_bias_tiles(rel_bias):
    return pl.pallas_call(
        _rel_bias_kernel,
        grid=(HEADS,),
        in_specs=[pl.BlockSpec(memory_space=pltpu.SMEM)],
        out_specs=pl.BlockSpec((1, 2, MOBA_BLOCK, MOBA_BLOCK), lambda h: (h, 0, 0, 0)),
        out_shape=jax.ShapeDtypeStruct((HEADS, 2, MOBA_BLOCK, MOBA_BLOCK), F32),
        compiler_params=_cparams("parallel"),
    )(rel_bias)


def _moba_attn_kernel(qt_ref, k_ref, vt_ref, sc_ref, bias_ref, o_ref, mask_ref, sa_ref, sb_ref,
                      pend_ref):
    blk = MOBA_BLOCK
    i = pl.program_id(1)
    nb = sc_ref.shape[1]
    heads = range(qt_ref.shape[0])
    dh = vt_ref.shape[1]
    bidx = lax.broadcasted_iota(jnp.int32, (nb, blk), 0)
    past = bidx < i
    ones_rows = jnp.ones((SUM_ROWS, blk), BF16)

    def block(j):
        return pl.ds(pl.multiple_of(j * blk, blk), blk)

    def scores(hh, j):
        return jnp.dot(k_ref[hh, block(j), :], qt_ref[hh], preferred_element_type=F32)

    def weighted_values(hh, j, p):
        vals = jnp.concatenate([vt_ref[hh, :, block(j)], ones_rows], axis=0)
        return jnp.dot(vals, p, preferred_element_type=F32)

    def softmax_step(s, m, mrow):
        m_new = jnp.maximum(m, jnp.max(s, axis=0, keepdims=True) + mrow)
        return m_new, jnp.exp2(m - m_new), jnp.exp2((s - (m_new - mrow)).astype(BF16))

    j_prev = jnp.maximum(i - 1, 0)
    s_own = [scores(hh, i) + bias_ref[hh, 0] for hh in heads]
    s_prev = [scores(hh, j_prev) + bias_ref[hh, 1] for hh in heads]
    n_far = jnp.maximum(i - 1, 0)
    last_far = jnp.maximum(n_far - 1, 0)

    def store_scores(s_ref, j):
        for hh in heads:
            s_ref[hh] = scores(hh, jnp.minimum(j, last_far))

    store_scores(sa_ref, 0)

    for hh in heads:
        scm = jnp.where(past, sc_ref[hh], -jnp.inf)
        sel = jnp.zeros((nb, blk), jnp.bool_)
        for _ in range(MOBA_TOPK):
            top = jnp.max(scm, axis=0, keepdims=True)
            idx = jnp.min(jnp.where(scm == top, bidx, nb), axis=0, keepdims=True)
            pick = bidx == idx
            sel = jnp.logical_or(sel, pick)
            scm = jnp.where(pick, -jnp.inf, scm)
        mask_ref[hh] = jnp.where(jnp.logical_and(sel, past), 0.0, NEG_BIG)

    ms, accs, pend = [], [], []
    for hh in heads:
        m = jnp.max(s_own[hh], axis=0, keepdims=True)
        accs.append(weighted_values(hh, i, jnp.exp2((s_own[hh] - m).astype(BF16))))
        ms.append(m)
    for hh in heads:
        mrow = jnp.where(i > 0, mask_ref[hh, pl.ds(j_prev, 1), :], NEG_BIG)
        ms[hh], alpha, pend_ref[hh] = softmax_step(s_prev[hh], ms[hh], mrow)
        pend.append(alpha)

    def far_body(t, carry):
        j_pend, ms, accs, pend = carry
        j0 = 2 * t
        j1 = jnp.minimum(j0 + 1, last_far)
        pv_pend = [weighted_values(hh, j_pend, pend_ref[hh]) for hh in heads]
        store_scores(sb_ref, j0 + 1)
        steps = [softmax_step(sa_ref[hh], ms[hh], mask_ref[hh, pl.ds(j0, 1), :]) for hh in heads]
        accs = [pend[hh] * accs[hh] + pv_pend[hh] for hh in heads]
        pv0 = [weighted_values(hh, j0, steps[hh][2]) for hh in heads]
        store_scores(sa_ref, j0 + 2)
        new_ms, new_pend = [], []
        for hh in heads:
            mrow = jnp.where(j0 + 1 < n_far, mask_ref[hh, pl.ds(j1, 1), :], NEG_BIG)
            m, alpha, pend_ref[hh] = softmax_step(sb_ref[hh], steps[hh][0], mrow)
            new_ms.append(m)
            new_pend.append(alpha)
        accs = [steps[hh][1] * accs[hh] + pv0[hh] for hh in heads]
        return j1, tuple(new_ms), tuple(accs), tuple(new_pend)

    j_pend, ms, accs, pend = lax.fori_loop(
        0, (n_far + 1) // 2, far_body, (j_prev, tuple(ms), tuple(accs), tuple(pend)))
    for hh in heads:
        acc = pend[hh] * accs[hh] + weighted_values(hh, j_pend, pend_ref[hh])
        o_ref[hh] = acc[:dh] / acc[dh:dh + 1]


def _moba_attn(qt, k, vt, sc_t, bias):
    bh, s, dh = k.shape
    nb = sc_t.shape[1]
    blk = MOBA_BLOCK
    hp = ATTN_HEADS_PER_STEP
    return pl.pallas_call(
        _moba_attn_kernel,
        grid=(bh // hp, s // blk),
        in_specs=[pl.BlockSpec((hp, dh, blk), lambda b, i: (b, 0, i)),
                  pl.BlockSpec((hp, s, dh), lambda b, i: (b, 0, 0)),
                  pl.BlockSpec((hp, dh, s), lambda b, i: (b, 0, 0)),
                  pl.BlockSpec((hp, nb, blk), lambda b, i: (b, 0, i)),
                  pl.BlockSpec((hp, 2, blk, blk), lambda b, i: (b % (HEADS // hp), 0, 0, 0))],
        out_specs=pl.BlockSpec((hp, dh, blk), lambda b, i: (b, 0, i)),
        out_shape=jax.ShapeDtypeStruct((bh, dh, s), F32),
        scratch_shapes=[pltpu.VMEM((hp, nb, blk), F32), pltpu.VMEM((hp, blk, blk), F32),
                        pltpu.VMEM((hp, blk, blk), F32), pltpu.VMEM((hp, blk, blk), BF16)],
        compiler_params=_cparams("parallel", "parallel"),
    )(qt, k, vt, sc_t, bias)


def _merge_router_kernel(x_ref, ys_ref, bonus_ref, g_ref, ym_ref, gr_ref, gm_ref,
                         lnx_ref, ones_ref, wr_ref, wm_ref, wo_ref, gf_ref, wrt_ref, brt_ref,
                         xo_ref, xn_ref, route_ref):
    ones_bd = ones_ref[...]
    inv_n = 1.0 / HEAD_DIM
    y = _load_heads(ys_ref)
    mean = _dot_lsplit(y, ones_bd) * inv_n
    d = y - mean
    var = _dot_lsplit(d * d, ones_bd) * inv_n
    yr = d * lax.rsqrt(var + LN_X_EPS) * lnx_ref[0:1, :] + lnx_ref[1:2, :]
    yr = (yr + bonus_ref[...]) * g_ref[...]
    h = (_sigmoid(gr_ref[...]) * _dot(yr, wr_ref[...])
         + _sigmoid(gm_ref[...]) * _dot(ym_ref[0].T, wm_ref[...]))
    x = x_ref[...] + _dot(h, wo_ref[...])
    xo_ref[...] = x

    ms = jnp.mean(x * x, axis=-1, keepdims=True)
    xn = x * lax.rsqrt(ms + NORM_EPS) * gf_ref[...]
    xn_ref[...] = xn

    logits = _dot_x3(xn, wrt_ref[...]) + brt_ref[...]
    lane = lax.broadcasted_iota(jnp.int32, logits.shape, 1)
    big = logits.shape[1]
    is_grp = lane < N_GROUPS
    lg = jnp.where(is_grp, logits, -jnp.inf)
    mg = jnp.max(lg, axis=1, keepdims=True)
    g_sel = jnp.min(jnp.where(lg == mg, lane, big), axis=1, keepdims=True)
    p_top = 1.0 / jnp.sum(jnp.where(is_grp, jnp.exp(logits - mg), 0.0), axis=1, keepdims=True)
    lo_lane = N_GROUPS + g_sel * EXPERTS_PER_GROUP
    in_grp = jnp.logical_and(lane >= lo_lane, lane < lo_lane + EXPERTS_PER_GROUP)
    le = jnp.where(in_grp, logits, -jnp.inf)
    m1 = jnp.max(le, axis=1, keepdims=True)
    i1 = jnp.min(jnp.where(le == m1, lane, big), axis=1, keepdims=True)
    le2 = jnp.where(lane == i1, -jnp.inf, le)
    m2 = jnp.max(le2, axis=1, keepdims=True)
    i2 = jnp.min(jnp.where(le2 == m2, lane, big), axis=1, keepdims=True)
    e2 = jnp.exp(m2 - m1)
    gate1 = p_top / (1.0 + e2)
    gate2 = p_top * e2 / (1.0 + e2)
    route = jnp.where(lane == 0, (i1 - N_GROUPS).astype(F32),
                      jnp.where(lane == 1, (i2 - N_GROUPS).astype(F32),
                                jnp.where(lane == 2, gate1, jnp.where(lane == 3, gate2, 0.0))))
    route_ref[...] = route


def _merge_router(x, ys, bonus, g, ym_t, proj, lnx, ones_bd, wr, wm, wo, gf, wrt, brt):
    t, d = x.shape
    tm = MERGE_TILE
    tps = ys.shape[2] // tm
    row = lambda w: pl.BlockSpec((tm, w), lambda i: (i, 0))
    full = lambda shape: pl.BlockSpec(shape, lambda i: tuple(0 for _ in shape))
    return pl.pallas_call(
        _merge_router_kernel,
        grid=(t // tm,),
        in_specs=[row(d),
                  pl.BlockSpec((1, HEADS, tm, HEAD_DIM), lambda i: (i // tps, 0, i % tps, 0)),
                  row(WIDTH), row(WIDTH),
                  pl.BlockSpec((1, WIDTH, tm), lambda i: (i // tps, 0, i % tps)),
                  pl.BlockSpec((tm, d), lambda i: (i, 3)),
                  pl.BlockSpec((tm, d), lambda i: (i, 4)),
                  full((2, WIDTH)), full((WIDTH, WIDTH)), full((WIDTH, d)), full((WIDTH, d)),
                  full((d, d)), full((1, d)), full((d, 128)), full((1, 128))],
        out_specs=[row(d), row(d), row(128)],
        out_shape=[jax.ShapeDtypeStruct((t, d), F32), jax.ShapeDtypeStruct((t, d), F32),
                   jax.ShapeDtypeStruct((t, 128), F32)],
        compiler_params=_cparams("parallel"),
    )(x, ys, bonus, g, ym_t, proj, proj, lnx, ones_bd, wr, wm, wo, gf, wrt, brt)


def _moe_dispatch_kernel(pos_ref, x_ref, xs_in, xs_out, sem):
    del xs_in
    i = pl.program_id(0)
    tm = x_ref.shape[0]

    def row_copy(r, j):
        p = pos_ref[(i * tm + r) * 2 + j]
        return pltpu.make_async_copy(x_ref.at[pl.ds(r, 1), :], xs_out.at[pl.ds(p, 1), :], sem)

    def start(r, c):
        row_copy(r, 0).start(priority=0)
        row_copy(r, 1).start(priority=1)
        return c

    def wait(r, c):
        row_copy(r, 0).wait()
        row_copy(r, 1).wait()
        return c

    lax.fori_loop(0, tm, start, 0, unroll=DMA_UNROLL)
    lax.fori_loop(0, tm, wait, 0, unroll=DMA_UNROLL)


def _moe_dispatch(pos, xn, n_rows):
    t, d = xn.shape
    tm = MOE_ROW_TILE
    grid_spec = pltpu.PrefetchScalarGridSpec(
        num_scalar_prefetch=1,
        grid=(t // tm,),
        in_specs=[pl.BlockSpec((tm, d), lambda i, p: (i, 0)),
                  pl.BlockSpec(memory_space=pl.ANY)],
        out_specs=pl.BlockSpec(memory_space=pl.ANY),
        scratch_shapes=[pltpu.SemaphoreType.DMA(())],
    )
    return pl.pallas_call(
        _moe_dispatch_kernel,
        grid_spec=grid_spec,
        out_shape=jax.ShapeDtypeStruct((n_rows, d), F32),
        input_output_aliases={2: 0},
        compiler_params=_cparams("arbitrary"),
    )(pos, xn, jnp.zeros((n_rows, d), F32))


def _moe_ffn_kernel(texp_ref, nused_ref, x_ref, wg_ref, wu_ref, wd_ref, o_ref, wg_bf, wu_bf, wd_bf):
    i = pl.program_id(0)
    used = i < nused_ref[0]
    new_expert = jnp.logical_or(i == 0, texp_ref[i] != texp_ref[jnp.maximum(i - 1, 0)])

    @pl.when(jnp.logical_and(used, new_expert))
    def _():
        wg_bf[...] = wg_ref[0, 0].astype(BF16)
        wu_bf[...] = wu_ref[0, 0].astype(BF16)
        wd_bf[...] = wd_ref[0, 0].astype(BF16)

    @pl.when(used)
    def _():
        xb = x_ref[...].astype(BF16)
        hg = jnp.dot(xb, wg_bf[...], preferred_element_type=F32)
        hu = jnp.dot(xb, wu_bf[...], preferred_element_type=F32)
        h = hg * _sigmoid(hg) * hu
        o_ref[...] = jnp.dot(h.astype(BF16), wd_bf[...], preferred_element_type=F32)

    @pl.when(jnp.logical_not(used))
    def _():
        o_ref[...] = jnp.zeros_like(o_ref)


def _moe_ffn(tile_expert, nused, xs, wg, wu, wd, layer):
    n_rows, d = xs.shape
    tm = MOE_TILE
    dh = wg.shape[3]
    grid_spec = pltpu.PrefetchScalarGridSpec(
        num_scalar_prefetch=2,
        grid=(n_rows // tm,),
        in_specs=[pl.BlockSpec((tm, d), lambda i, te, nu: (jnp.minimum(i, nu[0] - 1), 0)),
                  pl.BlockSpec((1, 1, d, dh), lambda i, te, nu: (layer, te[i], 0, 0)),
                  pl.BlockSpec((1, 1, d, dh), lambda i, te, nu: (layer, te[i], 0, 0)),
                  pl.BlockSpec((1, 1, dh, d), lambda i, te, nu: (layer, te[i], 0, 0))],
        out_specs=pl.BlockSpec((tm, d), lambda i, te, nu: (i, 0)),
        scratch_shapes=[pltpu.VMEM((d, dh), BF16), pltpu.VMEM((d, dh), BF16),
                        pltpu.VMEM((dh, d), BF16)],
    )
    return pl.pallas_call(
        _moe_ffn_kernel,
        grid_spec=grid_spec,
        out_shape=jax.ShapeDtypeStruct((n_rows, d), F32),
        compiler_params=_cparams("arbitrary"),
    )(tile_expert, nused, xs, wg, wu, wd)


def _moe_combine_kernel(pos_ref, ys_hbm, x_ref, route_ref, o_ref, buf0, buf1, sem):
    i = pl.program_id(0)
    tm = buf0.shape[0]

    def row_copy(r, j, buf):
        p = pos_ref[(i * tm + r) * 2 + j]
        return pltpu.make_async_copy(ys_hbm.at[pl.ds(p, 1), :], buf.at[pl.ds(r, 1), :], sem)

    def start(r, c):
        row_copy(r, 0, buf0).start(priority=0)
        row_copy(r, 1, buf1).start(priority=1)
        return c

    def wait(r, c):
        row_copy(r, 0, buf0).wait()
        row_copy(r, 1, buf1).wait()
        return c

    lax.fori_loop(0, tm, start, 0, unroll=DMA_UNROLL)
    lax.fori_loop(0, tm, wait, 0, unroll=DMA_UNROLL)
    route = route_ref[...]
    o_ref[...] = x_ref[...] + route[:, 2:3] * buf0[...] + route[:, 3:4] * buf1[...]


def _moe_combine(pos, ys, x, route):
    t, d = x.shape
    tm = MOE_ROW_TILE
    grid_spec = pltpu.PrefetchScalarGridSpec(
        num_scalar_prefetch=1,
        grid=(t // tm,),
        in_specs=[pl.BlockSpec(memory_space=pl.ANY),
                  pl.BlockSpec((tm, d), lambda i, p: (i, 0)),
                  pl.BlockSpec((tm, 128), lambda i, p: (i, 0))],
        out_specs=pl.BlockSpec((tm, d), lambda i, p: (i, 0)),
        scratch_shapes=[pltpu.VMEM((tm, d), F32), pltpu.VMEM((tm, d), F32),
                        pltpu.SemaphoreType.DMA(())],
    )
    return pl.pallas_call(
        _moe_combine_kernel,
        grid_spec=grid_spec,
        out_shape=jax.ShapeDtypeStruct((t, d), F32),
        compiler_params=_cparams("arbitrary"),
    )(pos, ys, x, route)


def _moe_plan(route):
    tm = MOE_TILE
    e_flat = route[:, 0:2].astype(jnp.int32).reshape(-1)
    n_rows = e_flat.shape[0] + N_EXPERTS * tm
    onehot = (e_flat[:, None] == jnp.arange(N_EXPERTS, dtype=jnp.int32)[None, :]).astype(jnp.int32)
    running = jnp.cumsum(onehot, axis=0)
    counts = running[-1]
    padded = ((counts + tm - 1) // tm) * tm
    pend = jnp.cumsum(padded)
    pstart = pend - padded
    pos = jnp.sum(onehot * (running - 1 + pstart[None, :]), axis=1).astype(jnp.int32)
    tile_start = jnp.arange(n_rows // tm, dtype=jnp.int32) * tm
    tile_expert = jnp.minimum(jnp.sum(pend[None, :] <= tile_start[:, None], axis=1),
                              N_EXPERTS - 1).astype(jnp.int32)
    nused = (pend[-1:] // tm).astype(jnp.int32)
    return tile_expert, nused, pos, n_rows


def _pad_cols(w, width):
    return jnp.pad(w, ((0, 0), (0, width - w.shape[1])))


def _pad_rows(w, rows):
    return jnp.pad(w, ((0, rows - w.shape[0]), (0, 0)))


def kernel(x, norm_mix_g, norm_ffn_g, w_in, mu_rkv, mu_x, w0, w1, w2, a0, a1, a2, g1, g2, k_k, k_a, r_k, lnx_g, lnx_b, vres_mu, vres_v0, vres_v1, vres_v2, q_gain, k_gain, rel_bias, w_br_rwkv, w_br_moba, w_out, w_grp, b_grp, w_exp, b_exp, w_gate, w_up, w_down):
    batch, seq, d_model = x.shape
    depth = w_in.shape[0]
    t = batch * seq
    nb = seq // MOBA_BLOCK
    assert seq % MOBA_BLOCK == 0 and seq % SCAN_BLOCK == 0 and (batch * HEADS) % SCAN_GROUP == 0

    head_of = jnp.arange(WIDTH) // HEAD_DIM
    ones_bd = (head_of[:, None] == head_of[None, :]).astype(BF16)
    bias_tiles = _rel_bias_tiles(rel_bias)
    gains = jnp.stack([jnp.tile(q_gain, (1, HEADS)), jnp.tile(k_gain, (1, HEADS))], axis=1)

    xf = x.reshape(t, d_model)
    v_first = None
    for l in range(depth):
        has_vres = l > 0
        lora_w = [w1[l], a1[l], g1[l]]
        lora_mu = [mu_x[l, 0], mu_x[l, 1], mu_x[l, 2]]
        if has_vres:
            lora_w.append(vres_v1[l - 1])
            lora_mu.append(vres_mu[l - 1])
        p1 = _pad_cols(jnp.concatenate([w * (1.0 - m)[:, None] for w, m in zip(lora_w, lora_mu)], 1),
                       LORA_PAD)
        p2 = _pad_cols(jnp.concatenate([w * m[:, None] for w, m in zip(lora_w, lora_mu)], 1), LORA_PAD)
        w_all = jnp.concatenate([w_in[l], p1, p2], axis=1).astype(BF16)
        wwa = jnp.concatenate([
            jnp.concatenate([w2[l], jnp.zeros((DECAY_LORA, WIDTH), F32)], 1),
            jnp.concatenate([jnp.zeros((AAA_LORA, WIDTH), F32), a2[l]], 1)], 0).astype(BF16)
        v2p = (_pad_rows(vres_v2[l - 1], 128) if has_vres else jnp.zeros((128, WIDTH), F32)).astype(BF16)
        v0 = vres_v0[l - 1] if has_vres else jnp.zeros((WIDTH,), F32)
        vecs = jnp.stack([w0[l], a0[l], k_k[l], k_a[l], r_k[l].reshape(-1), v0,
                          jnp.zeros((WIDTH,), F32), jnp.zeros((WIDTH,), F32)])

        proj = _norm_proj(xf, norm_mix_g[l], w_all)

        *heads, v_row, g, bonus = _rwkv_prep(proj, seq, mu_rkv[l], vecs, wwa,
                                             g2[l].astype(BF16), v2p, ones_bd, v_first)
        if l == 0:
            v_first = v_row
        bh = batch * HEADS
        y_scan = _rwkv_scan(*[a_.reshape(bh, seq, HEAD_DIM) for a_ in heads]).reshape(
            batch, HEADS, seq, HEAD_DIM)

        qf, q_t, k_h, v_t, kmean = _moba_prep(proj, seq, gains[l], ones_bd)
        km = kmean.reshape(batch, nb, HEADS, HEAD_DIM)
        km_t = jnp.einsum("bjhd,hg->bgjhd", km, jnp.eye(HEADS, dtype=F32)).reshape(
            batch, HEADS * nb, WIDTH)
        sc_t = _gating_scores(qf, km_t, batch).reshape(bh, nb, seq)
        y_m = _moba_attn(q_t.reshape(bh, HEAD_DIM, seq), k_h.reshape(bh, seq, HEAD_DIM),
                         v_t.reshape(bh, HEAD_DIM, seq), sc_t, bias_tiles).reshape(
                             batch, WIDTH, seq)

        wrt = _pad_cols(jnp.concatenate([w_grp[l], w_exp[l]], axis=1), 128)
        brt = _pad_cols(jnp.concatenate([b_grp[l], b_exp[l]])[None, :], 128)
        lnx = jnp.stack([lnx_g[l], lnx_b[l]])
        xf, xn2, route = _merge_router(
            xf, y_scan, bonus, g, y_m, proj, lnx, ones_bd,
            w_br_rwkv[l].astype(BF16), w_br_moba[l].astype(BF16), w_out[l].astype(BF16),
            norm_ffn_g[l].reshape(1, d_model), wrt, brt)

        tile_expert, nused, pos, n_rows = _moe_plan(route)
        xs = _moe_dispatch(pos, xn2, n_rows)
        ys = _moe_ffn(tile_expert, nused, xs, w_gate, w_up, w_down, l)
        xf = _moe_combine(pos, ys, xf, route)
    return xf.reshape(batch, seq, d_model)
```
